```python
import math
import jax
import jax.numpy as jnp
from jax import lax
import numpy as np

D_MODEL = 1024
BATCH = 2
SEQ = 16384
DEPTH = 4
DEC_BATCH = 8
DEC_SEQ = 32
PAST_LEN = 1024

CHUNK = 64
N_MIXERS = 4
N_LAYERS_A = (DEPTH + 3) // 4
N_LAYERS_B = (DEPTH + 2) // 4
N_LAYERS_C = (DEPTH + 1) // 4
N_LAYERS_D = DEPTH // 4
ALPHA = (2.0 * DEPTH) ** 0.25
BETA = (8.0 * DEPTH) ** -0.25
LN_EPS = 1e-5
NEG_INF = -1e30

A_HEADS = 8
A_KV_HEADS = 2
A_HEAD_DIM = 128
A_GROUP = A_HEADS // A_KV_HEADS
IDX_HEADS = 8
IDX_DIM = 64
IDX_SCALE = (IDX_DIM ** -0.5) * (IDX_HEADS ** -0.5)
TOPK_MAX = 256
Q_BLOCK = 128
ROPE_THETA = 10000.0
A_SPLITS = (A_HEADS * A_HEAD_DIM, A_KV_HEADS * A_HEAD_DIM, A_KV_HEADS * A_HEAD_DIM,
            IDX_HEADS * IDX_DIM, IDX_DIM, IDX_HEADS)
A_PROJ = sum(A_SPLITS)

RNN_WIDTH = 1408
LRU_BLOCKS = 8
LRU_BLOCK_DIM = RNN_WIDTH // LRU_BLOCKS
B_CONV = 4
LRU_C = 8.0

C_WIDTH = 1024
C_CONV = 31

D_HEAD_DIM = 64
D_HEADS = D_MODEL // D_HEAD_DIM
DECAY_LORA = 64
AAA_LORA = 64
GATE_LORA = 128
GN_EPS = 64e-5

D_FF = 2816
F_CONV = 3

kernel_name = 'hybrid_streaming_encoder_step'


def _split_last(z, sizes):
    return jnp.split(z, np.cumsum(sizes)[:-1].tolist(), axis=-1)


def layer_norm(x, g, b, eps=LN_EPS):
    xf = x.astype(jnp.float32)
    mu = jnp.mean(xf, axis=-1, keepdims=True)
    var = jnp.mean(jnp.square(xf - mu), axis=-1, keepdims=True)
    return ((xf - mu) * lax.rsqrt(var + eps) * g + b).astype(x.dtype)


def causal_dwconv(x, buf, w, b):
    width = w.shape[0]
    xp = jnp.concatenate([buf.astype(x.dtype), x], axis=1)
    y = lax.conv_general_dilated(xp, w[:, None, :].astype(x.dtype), window_strides=(1,), padding='VALID',
                                 dimension_numbers=('NWC', 'WIO', 'NWC'), feature_group_count=x.shape[-1])
    return (y + b).astype(x.dtype), xp[:, xp.shape[1] - (width - 1):]


def rope(x, pos):
    half = x.shape[-1] // 2
    inv_freq = jnp.exp(-math.log(ROPE_THETA) * jnp.arange(half, dtype=jnp.float32) / half)
    ang = pos.astype(jnp.float32)[:, None] * inv_freq[None, :]
    cos = jnp.cos(ang)[None, :, None, :]
    sin = jnp.sin(ang)[None, :, None, :]
    xf = x.astype(jnp.float32)
    x1, x2 = xf[..., :half], xf[..., half:]
    return jnp.concatenate([x1 * cos - x2 * sin, x2 * cos + x1 * sin], axis=-1).astype(x.dtype)


def dsa_project(x, pos, w_in):
    B, T, _ = x.shape
    q, k, v, qi, ki, wi = _split_last(x @ w_in, A_SPLITS)
    q = rope(q.reshape(B, T, A_HEADS, A_HEAD_DIM), pos)
    k = rope(k.reshape(B, T, A_KV_HEADS, A_HEAD_DIM), pos)
    v = v.reshape(B, T, A_KV_HEADS, A_HEAD_DIM)
    qi = rope(qi.reshape(B, T, IDX_HEADS, IDX_DIM), pos)
    ki = rope(ki[:, :, None, :], pos)[:, :, 0, :]
    return q, k, v, qi, ki, wi


def dsa_block(q, qi, wi, q_pos, k, v, ki, k_pos, topk):
    f32 = jnp.float32
    B, Tq = q.shape[:2]
    rel = jnp.einsum('bqhd,bsd->bqhs', qi.astype(f32), ki.astype(f32))
    score = jnp.einsum('bqh,bqhs->bqs', wi.astype(f32), jax.nn.relu(rel)) * IDX_SCALE
    admissible = (k_pos[None, :] // CHUNK) <= (q_pos[:, None] // CHUNK)
    score = jnp.where(admissible[None], score, NEG_INF)
    _, idx = lax.top_k(score, topk)
    valid = (k_pos[idx] // CHUNK) <= (q_pos[None, :, None] // CHUNK)
    kg = jax.vmap(lambda kb, ib: kb[ib])(k, idx)
    vg = jax.vmap(lambda vb, ib: vb[ib])(v, idx)
    qg = q.reshape(B, Tq, A_KV_HEADS, A_GROUP, A_HEAD_DIM).astype(f32)
    logits = jnp.einsum('bqhgd,bqkhd->bqhgk', qg, kg.astype(f32)) * (A_HEAD_DIM ** -0.5)
    logits = jnp.where(valid[:, :, None, None, :], logits, NEG_INF)
    p = jax.nn.softmax(logits, axis=-1)
    o = jnp.einsum('bqhgk,bqkhd->bqhgd', p, vg.astype(f32))
    return o.reshape(B, Tq, A_HEADS * A_HEAD_DIM).astype(q.dtype)


def mixer_a_prompt(x, w_in, w_out):
    B, T, _ = x.shape
    pos = jnp.arange(T, dtype=jnp.int32)
    q, k, v, qi, ki, wi = dsa_project(x, pos, w_in)
    topk = min(TOPK_MAX, T // 4)
    nb = T // Q_BLOCK

    def to_blocks(a):
        return jnp.moveaxis(a.reshape((B, nb, Q_BLOCK) + a.shape[2:]), 1, 0)

    def body(args):
        qb, qib, wib, pb = args
        return dsa_block(qb, qib, wib, pb, k, v, ki, pos, topk)

    o = lax.map(body, (to_blocks(q), to_blocks(qi), to_blocks(wi), pos.reshape(nb, Q_BLOCK)))
    o = jnp.moveaxis(o, 0, 1).reshape(B, T, A_HEADS * A_HEAD_DIM)
    return o @ w_out, (k, v, ki)


def mixer_a_sample(x, cache_k, cache_v, cache_ki, w_in, w_out):
    B, T, _ = x.shape
    P = cache_k.shape[1]
    pos = P + jnp.arange(T, dtype=jnp.int32)
    q, k, v, qi, ki, wi = dsa_project(x, pos, w_in)
    k_all = jnp.concatenate([cache_k.astype(k.dtype), k], axis=1)
    v_all = jnp.concatenate([cache_v.astype(v.dtype), v], axis=1)
    ki_all = jnp.concatenate([cache_ki.astype(ki.dtype), ki], axis=1)
    k_pos = jnp.arange(P + T, dtype=jnp.int32)
    topk = min(TOPK_MAX, (P + T) // 4)
    o = dsa_block(q, qi, wi, pos, k_all, v_all, ki_all, k_pos, topk)
    return o @ w_out, (k, v, ki)


def linear_scan(a, b, h0):
    def combine(l, r):
        al, bl = l
        ar, br = r
        return al * ar, ar * bl + br
    a_cum, b_cum = lax.associative_scan(combine, (a, b), axis=1)
    return a_cum * h0[:, None, :] + b_cum


def mixer_b(x, h0, conv_buf, w_in, conv_w, conv_b, gate_a_w, gate_a_b, gate_x_w, gate_x_b, lam, w_out):
    f32 = jnp.float32
    B, T, _ = x.shape
    gate_branch, rec_branch = _split_last(x @ w_in, (RNN_WIDTH, RNN_WIDTH))
    u, new_buf = causal_dwconv(rec_branch, conv_buf, conv_w, conv_b)
    uf = u.astype(f32)
    ub = uf.reshape(B, T, LRU_BLOCKS, LRU_BLOCK_DIM)
    r = jax.nn.sigmoid(jnp.einsum('btnd,nde->btne', ub, gate_a_w.astype(f32)).reshape(B, T, RNN_WIDTH) + gate_a_b)
    ig = jax.nn.sigmoid(jnp.einsum('btnd,nde->btne', ub, gate_x_w.astype(f32)).reshape(B, T, RNN_WIDTH) + gate_x_b)
    log_a = -LRU_C * r * jax.nn.softplus(-lam.astype(f32))
    a = jnp.exp(log_a)
    b = jnp.sqrt(-jnp.expm1(2.0 * log_a)) * (ig * uf)
    h = linear_scan(a, b, h0.astype(f32))
    y = jax.nn.gelu(gate_branch.astype(f32)) * h
    return y.astype(x.dtype) @ w_out, (h[:, -1], new_buf)


def mixer_c(x, conv_buf, w_in, b_in, conv_w, conv_b, ln_g, ln_b, w_out, b_out):
    z = x @ w_in + b_in
    u = z[..., :C_WIDTH] * jax.nn.sigmoid(z[..., C_WIDTH:])
    c, new_buf = causal_dwconv(u, conv_buf, conv_w, conv_b)
    c = layer_norm(c, ln_g, ln_b)
    return jax.nn.silu(c) @ w_out + b_out, new_buf


def mixer_d(x, wkv0, shift0, mu, w_r, w_k, w_v, w_o, w0, w1, w2, a0, a1, a2, g1, g2, k_k, k_a, r_k, gn_g, gn_b):
    f32 = jnp.float32
    B, T, C = x.shape
    x_prev = jnp.concatenate([shift0.astype(x.dtype)[:, None, :], x[:, :-1]], axis=1)
    xx = x_prev - x
    xr, xw, xk, xv, xa, xg = (x + xx * mu[n] for n in range(6))
    r = (xr @ w_r).astype(f32)
    k = (xk @ w_k).astype(f32)
    v = (xv @ w_v).astype(f32)
    w_log = -jax.nn.softplus(-(w0 + jnp.tanh(xw @ w1) @ w2).astype(f32)) - 0.5
    decay = jnp.exp(-jnp.exp(w_log))
    a = jax.nn.sigmoid((a0 + (xa @ a1) @ a2).astype(f32))
    g = (jax.nn.sigmoid(xg @ g1) @ g2).astype(f32)

    def heads(t):
        return t.reshape(B, T, D_HEADS, D_HEAD_DIM)

    kk = heads(k * k_k)
    kk = kk / jnp.maximum(jnp.linalg.norm(kk, axis=-1, keepdims=True), 1e-12)
    k = k * (1.0 + (a - 1.0) * k_a)
    r, k, v, decay, a = heads(r), heads(k), heads(v), heads(decay), heads(a)

    def step(S, inp):
        r_t, w_t, k_t, v_t, kk_t, ka_t = inp
        sa = jnp.einsum('bhij,bhj->bhi', S, -kk_t)
        S = S * w_t[:, :, None, :] + sa[..., None] * ka_t[:, :, None, :] + v_t[..., None] * k_t[:, :, None, :]
        return S, jnp.einsum('bhij,bhj->bhi', S, r_t)

    seq_in = tuple(jnp.moveaxis(t, 1, 0) for t in (r, decay, k, v, kk, kk * a))
    s_final, y = lax.scan(step, wkv0.astype(f32), seq_in)
    y = jnp.moveaxis(y, 0, 1)
    mu_y = jnp.mean(y, axis=-1, keepdims=True)
    var_y = jnp.mean(jnp.square(y - mu_y), axis=-1, keepdims=True)
    y = ((y - mu_y) * lax.rsqrt(var_y + GN_EPS)).reshape(B, T, C) * gn_g + gn_b
    bonus = jnp.sum(r * k * r_k, axis=-1, keepdims=True) * v
    y = (y + bonus.reshape(B, T, C)) * g
    return y.astype(x.dtype) @ w_o, (s_final, x[:, -1])


def conv_ffn(x, buf, w_up, conv_w, conv_b, w_down):
    u, new_buf = causal_dwconv(x @ w_up, buf, conv_w, conv_b)
    gate, val = _split_last(u, (D_FF, D_FF))
    return (jax.nn.silu(gate) * val) @ w_down, new_buf


def setup_inputs(seed: int = 0) -> dict:
    key = jax.random.key(seed)
    ks = iter(jax.random.split(key, 96))
    f32 = jnp.float32
    D = D_MODEL
    NA, NB, NC, ND = N_LAYERS_A, N_LAYERS_B, N_LAYERS_C, N_LAYERS_D

    def nrm(shape, scale=1.0):
        return scale * jax.random.normal(next(ks), shape, f32)

    def uni(shape, lo, hi):
        return jax.random.uniform(next(ks), shape, f32, lo, hi)

    a_init = uni((NB, RNN_WIDTH), 0.9, 0.999)
    s_init = a_init ** (1.0 / LRU_C)
    b_lambda = jnp.log(s_init) - jnp.log1p(-s_init)
    return {
        'x_prompt': nrm((BATCH, SEQ, D)),
        'x_sample': nrm((DEC_BATCH, DEC_SEQ, D)),
        'cache_a_k': nrm((NA, DEC_BATCH, PAST_LEN, A_KV_HEADS, A_HEAD_DIM)),
        'cache_a_v': nrm((NA, DEC_BATCH, PAST_LEN, A_KV_HEADS, A_HEAD_DIM)),
        'cache_a_kidx': nrm((NA, DEC_BATCH, PAST_LEN, IDX_DIM)),
        'state_b_h': nrm((NB, DEC_BATCH, RNN_WIDTH), 0.5),
        'state_b_conv': nrm((NB, DEC_BATCH, B_CONV - 1, RNN_WIDTH)),
        'state_c_conv': nrm((NC, DEC_BATCH, C_CONV - 1, C_WIDTH), 0.5),
        'state_d_wkv': nrm((ND, DEC_BATCH, D_HEADS, D_HEAD_DIM, D_HEAD_DIM), 0.3),
        'state_d_shift': nrm((ND, DEC_BATCH, D)),
        'state_f_conv': nrm((DEPTH, DEC_BATCH, F_CONV - 1, 2 * D_FF)),
        'ln_g': 1.0 + nrm((DEPTH, 2, D), 0.02),
        'ln_b': nrm((DEPTH, 2, D), 0.02),
        'a_w_in': nrm((NA, D, A_PROJ), D ** -0.5),
        'a_w_out': nrm((NA, A_HEADS * A_HEAD_DIM, D), BETA * (A_HEADS * A_HEAD_DIM) ** -0.5),
        'b_w_in': nrm((NB, D, 2 * RNN_WIDTH), D ** -0.5),
        'b_conv_w': nrm((NB, B_CONV, RNN_WIDTH), B_CONV ** -0.5),
        'b_conv_b': nrm((NB, RNN_WIDTH), 0.02),
        'b_gate_a_w': nrm((NB, LRU_BLOCKS, LRU_BLOCK_DIM, LRU_BLOCK_DIM), LRU_BLOCK_DIM ** -0.5),
        'b_gate_a_b': nrm((NB, RNN_WIDTH), 0.02),
        'b_gate_x_w': nrm((NB, LRU_BLOCKS, LRU_BLOCK_DIM, LRU_BLOCK_DIM), LRU_BLOCK_DIM ** -0.5),
        'b_gate_x_b': nrm((NB, RNN_WIDTH), 0.02),
        'b_lambda': b_lambda,
        'b_w_out': nrm((NB, RNN_WIDTH, D), BETA * RNN_WIDTH ** -0.5),
        'c_w_in': nrm((NC, D, 2 * C_WIDTH), D ** -0.5),
        'c_b_in': nrm((NC, 2 * C_WIDTH), 0.02),
        'c_conv_w': nrm((NC, C_CONV, C_WIDTH), C_CONV ** -0.5),
        'c_conv_b': nrm((NC, C_WIDTH), 0.02),
        'c_ln_g': 1.0 + nrm((NC, C_WIDTH), 0.02),
        'c_ln_b': nrm((NC, C_WIDTH), 0.02),
        'c_w_out': nrm((NC, C_WIDTH, D), BETA * C_WIDTH ** -0.5),
        'c_b_out': nrm((NC, D), 0.02),
        'd_mu': uni((ND, 6, D), 0.0, 1.0),
        'd_w_r': nrm((ND, D, D), D ** -0.5),
        'd_w_k': nrm((ND, D, D), D ** -0.5),
        'd_w_v': nrm((ND, D, D), D ** -0.5),
        'd_w_o': nrm((ND, D, D), BETA * D ** -0.5),
        'd_w0': uni((ND, D), -6.0, -1.0),
        'd_w1': nrm((ND, D, DECAY_LORA), 0.1 * D ** -0.5),
        'd_w2': nrm((ND, DECAY_LORA, D), DECAY_LORA ** -0.5),
        'd_a0': nrm((ND, D), 0.1),
        'd_a1': nrm((ND, D, AAA_LORA), 0.1 * D ** -0.5),
        'd_a2': nrm((ND, AAA_LORA, D), AAA_LORA ** -0.5),
        'd_g1': nrm((ND, D, GATE_LORA), D ** -0.5),
        'd_g2': nrm((ND, GATE_LORA, D), GATE_LORA ** -0.5),
        'd_k_k': 0.85 + nrm((ND, D), 0.02),
        'd_k_a': 1.0 + nrm((ND, D), 0.02),
        'd_r_k': nrm((ND, D_HEADS, D_HEAD_DIM), 0.1),
        'd_gn_g': 1.0 + nrm((ND, D), 0.02),
        'd_gn_b': nrm((ND, D), 0.02),
        'f_w_up': nrm((DEPTH, D, 2 * D_FF), D ** -0.5),
        'f_conv_w': nrm((DEPTH, F_CONV, 2 * D_FF), F_CONV ** -0.5),
        'f_conv_b': nrm((DEPTH, 2 * D_FF), 0.02),
        'f_w_down': nrm((DEPTH, D_FF, D), BETA * D_FF ** -0.5),
    }


def reference(x_prompt, x_sample, cache_a_k, cache_a_v, cache_a_kidx, state_b_h, state_b_conv, state_c_conv,
              state_d_wkv, state_d_shift, state_f_conv, ln_g, ln_b, a_w_in, a_w_out,
              b_w_in, b_conv_w, b_conv_b, b_gate_a_w, b_gate_a_b, b_gate_x_w, b_gate_x_b, b_lambda, b_w_out,
              c_w_in, c_b_in, c_conv_w, c_conv_b, c_ln_g, c_ln_b, c_w_out, c_b_out,
              d_mu, d_w_r, d_w_k, d_w_v, d_w_o, d_w0, d_w1, d_w2, d_a0, d_a1, d_a2, d_g1, d_g2,
              d_k_k, d_k_a, d_r_k, d_gn_g, d_gn_b, f_w_up, f_conv_w, f_conv_b, f_w_down):
    xp, xs = x_prompt, x_sample
    bp = xp.shape[0]
    ak_p, ak_s, av_p, av_s, aki_p, aki_s = [], [], [], [], [], []
    bh_p, bh_s, bc_p, bc_s = [], [], [], []
    cc_p, cc_s = [], []
    dw_p, dw_s, dsh_p, dsh_s = [], [], [], []
    fc_p, fc_s = [], []
    for i in range(DEPTH):
        m, j = i % N_MIXERS, i // N_MIXERS
        if m == 0:
            yp, (kp, vp, kip) = mixer_a_prompt(xp, a_w_in[j], a_w_out[j])
            ys, (ksm, vsm, kism) = mixer_a_sample(xs, cache_a_k[j], cache_a_v[j], cache_a_kidx[j], a_w_in[j], a_w_out[j])
            ak_p.append(kp); ak_s.append(ksm)
            av_p.append(vp); av_s.append(vsm)
            aki_p.append(kip); aki_s.append(kism)
        elif m == 1:
            prm = (b_w_in[j], b_conv_w[j], b_conv_b[j], b_gate_a_w[j], b_gate_a_b[j],
                   b_gate_x_w[j], b_gate_x_b[j], b_lambda[j], b_w_out[j])
            yp, (hp, cp) = mixer_b(xp, jnp.zeros((bp, RNN_WIDTH), jnp.float32),
                                   jnp.zeros((bp, B_CONV - 1, RNN_WIDTH), xp.dtype), *prm)
            ys, (hs, cs) = mixer_b(xs, state_b_h[j], state_b_conv[j], *prm)
            bh_p.append(hp); bh_s.append(hs)
            bc_p.append(cp); bc_s.append(cs)
        elif m == 2:
            prm = (c_w_in[j], c_b_in[j], c_conv_w[j], c_conv_b[j], c_ln_g[j], c_ln_b[j], c_w_out[j], c_b_out[j])
            yp, cp = mixer_c(xp, jnp.zeros((bp, C_CONV - 1, C_WIDTH), xp.dtype), *prm)
            ys, cs = mixer_c(xs, state_c_conv[j], *prm)
            cc_p.append(cp); cc_s.append(cs)
        else:
            prm = (d_mu[j], d_w_r[j], d_w_k[j], d_w_v[j], d_w_o[j], d_w0[j], d_w1[j], d_w2[j],
                   d_a0[j], d_a1[j], d_a2[j], d_g1[j], d_g2[j], d_k_k[j], d_k_a[j], d_r_k[j], d_gn_g[j], d_gn_b[j])
            yp, (sp, shp) = mixer_d(xp, jnp.zeros((bp, D_HEADS, D_HEAD_DIM, D_HEAD_DIM), jnp.float32),
                                    jnp.zeros((bp, D_MODEL), xp.dtype), *prm)
            ys, (ss, shs) = mixer_d(xs, state_d_wkv[j], state_d_shift[j], *prm)
            dw_p.append(sp); dw_s.append(ss)
            dsh_p.append(shp); dsh_s.append(shs)
        xp = layer_norm(ALPHA * xp + yp, ln_g[i, 0], ln_b[i, 0])
        xs = layer_norm(ALPHA * xs + ys, ln_g[i, 0], ln_b[i, 0])
        fp, fbp = conv_ffn(xp, jnp.zeros((bp, F_CONV - 1, 2 * D_FF), xp.dtype), f_w_up[i], f_conv_w[i], f_conv_b[i], f_w_down[i])
        fs, fbs = conv_ffn(xs, state_f_conv[i], f_w_up[i], f_conv_w[i], f_conv_b[i], f_w_down[i])
        fc_p.append(fbp); fc_s.append(fbs)
        xp = layer_norm(ALPHA * xp + fp, ln_g[i, 1], ln_b[i, 1])
        xs = layer_norm(ALPHA * xs + fs, ln_g[i, 1], ln_b[i, 1])
    return (xp, xs,
            jnp.stack(ak_p), jnp.stack(ak_s), jnp.stack(av_p), jnp.stack(av_s),
            jnp.stack(aki_p), jnp.stack(aki_s),
            jnp.stack(bh_p), jnp.stack(bh_s), jnp.stack(bc_p), jnp.stack(bc_s),
            jnp.stack(cc_p), jnp.stack(cc_s),
            jnp.stack(dw_p), jnp.stack(dw_s), jnp.stack(dsh_p), jnp.stack(dsh_s),
            jnp.stack(fc_p), jnp.stack(fc_s))
```

```python
import functools
import math

import jax
import jax.numpy as jnp
from jax import lax
from jax.experimental import pallas as pl
from jax.experimental.pallas import tpu as pltpu

f32 = jnp.float32
bf16 = jnp.bfloat16
i32 = jnp.int32

D_MODEL = 1024
DEPTH = 4
CHUNK = 64
ALPHA = (2.0 * DEPTH) ** 0.25
LN_EPS = 1e-5
NEG_INF = -1e30

A_HEADS = 8
A_KV_HEADS = 2
A_HEAD_DIM = 128
A_GROUP = A_HEADS // A_KV_HEADS
IDX_HEADS = 8
IDX_DIM = 64
IDX_SCALE = (IDX_DIM ** -0.5) * (IDX_HEADS ** -0.5)
TOPK_MAX = 256
ROPE_THETA = 10000.0

RNN_WIDTH = 1408
LRU_BLOCKS = 8
LRU_BLOCK_DIM = RNN_WIDTH // LRU_BLOCKS
B_CONV = 4
LRU_C = 8.0

C_WIDTH = 1024
C_CONV = 31

D_HEAD_DIM = 64
D_HEADS = D_MODEL // D_HEAD_DIM
D_PAIRS = D_HEADS // 2
GN_EPS = 64e-5
LORA_PAD = 128

D_FF = 2816
F_CONV = 3

LANES = 128
SUBLANES = 8
VMEM_LIMIT_BYTES = 60 * 1024 * 1024
WKV_CHUNK = 64

INT_MIN = -(2 ** 31)
INT_MAX = 2 ** 31 - 1


def _cparams(sem):
    return pltpu.CompilerParams(dimension_semantics=sem, vmem_limit_bytes=VMEM_LIMIT_BYTES)


def _const_spec(shape):
    nd = len(shape)
    return pl.BlockSpec(shape, lambda *_: (0,) * nd, pipeline_mode=pl.Buffered(1))


def _mm(a, b):
    return jnp.dot(a.astype(bf16), b.astype(bf16), preferred_element_type=f32)


def _split(a):
    hi = a.astype(bf16)
    lo = (a - hi.astype(f32)).astype(bf16)
    return hi, lo


def _dg(a, b, dims):
    return lax.dot_general(a, b, (dims, ((), ())), preferred_element_type=f32)


def _dot3(a, b, dims=((1,), (0,))):
    ah, al = _split(a)
    bh, bl = _split(b)
    return _dg(ah, bh, dims) + (_dg(ah, bl, dims) + _dg(al, bh, dims))


def _dot2(a, b_exact, dims=((1,), (0,))):
    ah, al = _split(a)
    bb = b_exact.astype(bf16)
    return _dg(ah, bb, dims) + _dg(al, bb, dims)


def _dot2l(a_exact, b, dims=((1,), (0,))):
    bh, bl = _split(b)
    aa = a_exact.astype(bf16)
    return _dg(aa, bh, dims) + _dg(aa, bl, dims)


def _layer_norm(z, g, b, eps=LN_EPS):
    mu = jnp.mean(z, axis=-1, keepdims=True)
    zc = z - mu
    var = jnp.mean(zc * zc, axis=-1, keepdims=True)
    return zc * lax.rsqrt(var + eps) * g + b


def _res_ln(x, y, g, b):
    return _layer_norm(ALPHA * x + y, g, b)


def _shift_rows(h, carry_rows):
    n = len(carry_rows)
    row = lax.broadcasted_iota(i32, h.shape, 0)
    out = []
    for k in range(1, n + 1):
        hk = pltpu.roll(h, k, axis=0)
        for j in range(k):
            hk = jnp.where(row == j, carry_rows[n - (k - j)], hk)
        out.append(hk)
    return out


FFN_TN = 256


def _ffn_body(x_ref, buf_ref, wup_ref, cw_ref, cb_ref, wdn_ref, g_ref, b_ref,
              o_ref, nbuf_ref, carry_ref, *, tb):
    t = pl.program_id(1)

    @pl.when(t == 0)
    def _():
        carry_ref[6:8, :] = buf_ref[0]

    x = x_ref[0]
    xb = x.astype(bf16)
    acc = jnp.zeros((tb, D_MODEL), f32)
    for j in range(D_FF // FFN_TN):
        us = []
        for part in range(2):
            c0 = part * D_FF + j * FFN_TN
            h = jnp.dot(xb, wup_ref[:, c0:c0 + FFN_TN], preferred_element_type=f32)
            cm2 = carry_ref[6:7, c0:c0 + FFN_TN]
            cm1 = carry_ref[7:8, c0:c0 + FFN_TN]
            h1, h2 = _shift_rows(h, [cm2, cm1])
            u = (cw_ref[0:1, c0:c0 + FFN_TN] * h2 + cw_ref[1:2, c0:c0 + FFN_TN] * h1
                 + cw_ref[2:3, c0:c0 + FFN_TN] * h + cb_ref[:, c0:c0 + FFN_TN])
            carry_ref[6:8, c0:c0 + FFN_TN] = h[tb - 2:tb, :]
            us.append(u)
        act = jax.nn.silu(us[0]) * us[1]
        acc = acc + jnp.dot(act.astype(bf16), wdn_ref[j * FFN_TN:(j + 1) * FFN_TN, :],
                            preferred_element_type=f32)
    nbuf_ref[0] = carry_ref[6:8, :]
    o_ref[0] = _res_ln(x, acc, g_ref[...], b_ref[...])


def _conv_ffn(x, buf, w_up, conv_w, conv_b, w_down, g, b, tb):
    B, T, _ = x.shape
    nt = T // tb
    return pl.pallas_call(
        functools.partial(_ffn_body, tb=tb),
        grid=(B, nt),
        in_specs=[
            pl.BlockSpec((1, tb, D_MODEL), lambda bi, ti: (bi, ti, 0)),
            pl.BlockSpec((1, F_CONV - 1, 2 * D_FF), lambda bi, ti: (bi, 0, 0)),
            _const_spec((D_MODEL, 2 * D_FF)),
            _const_spec((F_CONV, 2 * D_FF)),
            _const_spec((1, 2 * D_FF)),
            _const_spec((D_FF, D_MODEL)),
            _const_spec((1, D_MODEL)),
            _const_spec((1, D_MODEL)),
        ],
        out_specs=[
            pl.BlockSpec((1, tb, D_MODEL), lambda bi, ti: (bi, ti, 0)),
            pl.BlockSpec((1, F_CONV - 1, 2 * D_FF), lambda bi, ti: (bi, 0, 0)),
        ],
        out_shape=[jax.ShapeDtypeStruct((B, T, D_MODEL), f32),
                   jax.ShapeDtypeStruct((B, F_CONV - 1, 2 * D_FF), f32)],
        scratch_shapes=[pltpu.VMEM((SUBLANES, 2 * D_FF), f32)],
        compiler_params=_cparams(("arbitrary", "arbitrary")),
        name="conv_ffn",
    )(x, buf, w_up, conv_w, conv_b, w_down, g, b)


A_QW = A_HEADS * A_HEAD_DIM
A_KW = A_KV_HEADS * A_HEAD_DIM
A_PROJ_PAD = A_QW + 2 * A_KW + IDX_HEADS * LANES + LANES + LANES


def _rope128(z, cos, sin):
    return z * cos + pltpu.roll(z, A_HEAD_DIM // 2, axis=1) * sin


def _rope64(z, cos, sin):
    lane = lax.broadcasted_iota(i32, z.shape, 1)
    rot = jnp.where((lane & 32) == 0, pltpu.roll(z, LANES - 32, axis=1), pltpu.roll(z, 32, axis=1))
    return z * cos + rot * sin


def _aproj_body(x_ref, w_ref, c128_ref, s128_ref, c64_ref, s64_ref,
                q_ref, k_ref, v_ref, kb_ref, vb_ref, qi_ref, ki_ref, kib_ref, wi_ref):
    xb = x_ref[...].astype(bf16)
    c128, s128 = c128_ref[...], s128_ref[...]
    c64, s64 = c64_ref[...], s64_ref[...]
    qscale = A_HEAD_DIM ** -0.5
    for h in range(A_HEADS):
        z = jnp.dot(xb, w_ref[:, h * LANES:(h + 1) * LANES], preferred_element_type=f32)
        q_ref[h] = (_rope128(z, c128, s128) * qscale).astype(bf16)
    for h in range(A_KV_HEADS):
        c0 = A_QW + h * LANES
        z = jnp.dot(xb, w_ref[:, c0:c0 + LANES], preferred_element_type=f32)
        kr = _rope128(z, c128, s128)
        k_ref[:, h * LANES:(h + 1) * LANES] = kr
        kb_ref[:, h * LANES:(h + 1) * LANES] = kr.astype(bf16)
        c1 = A_QW + A_KW + h * LANES
        vv = jnp.dot(xb, w_ref[:, c1:c1 + LANES], preferred_element_type=f32)
        v_ref[:, h * LANES:(h + 1) * LANES] = vv
        vb_ref[:, h * LANES:(h + 1) * LANES] = vv.astype(bf16)
    base = A_QW + 2 * A_KW
    for h in range(IDX_HEADS):
        z = jnp.dot(xb, w_ref[:, base + h * LANES: base + (h + 1) * LANES], preferred_element_type=f32)
        qi_ref[h] = _rope64(z, c64, s64).astype(bf16)
    base = base + IDX_HEADS * LANES
    z = jnp.dot(xb, w_ref[:, base:base + LANES], preferred_element_type=f32)
    kir = _rope64(z, c64, s64)
    ki_ref[...] = kir[:, :IDX_DIM]
    kib_ref[...] = kir.astype(bf16)
    base = base + LANES
    wi_ref[...] = jnp.dot(xb, w_ref[:, base:base + LANES], preferred_element_type=f32) * IDX_SCALE


def _a_project(x2d, w_pad, c128, s128, c64, s64, tb):
    R = x2d.shape[0]
    nblk = R // tb
    ntab = c128.shape[0] // tb
    tab = pl.BlockSpec((tb, LANES), lambda i: (i % ntab, 0))
    row = lambda w: pl.BlockSpec((tb, w), lambda i: (i, 0))
    hm = pl.BlockSpec((A_HEADS, tb, LANES), lambda i: (0, i, 0))
    return pl.pallas_call(
        _aproj_body,
        grid=(nblk,),
        in_specs=[row(D_MODEL), _const_spec((D_MODEL, A_PROJ_PAD)), tab, tab, tab, tab],
        out_specs=[hm, row(A_KW), row(A_KW), row(A_KW), row(A_KW), hm, row(IDX_DIM), row(LANES), row(LANES)],
        out_shape=[
            jax.ShapeDtypeStruct((A_HEADS, R, LANES), bf16),
            jax.ShapeDtypeStruct((R, A_KW), f32),
            jax.ShapeDtypeStruct((R, A_KW), f32),
            jax.ShapeDtypeStruct((R, A_KW), bf16),
            jax.ShapeDtypeStruct((R, A_KW), bf16),
            jax.ShapeDtypeStruct((IDX_HEADS, R, LANES), bf16),
            jax.ShapeDtypeStruct((R, IDX_DIM), f32),
            jax.ShapeDtypeStruct((R, LANES), bf16),
            jax.ShapeDtypeStruct((R, LANES), f32),
        ],
        compiler_params=_cparams(("arbitrary",)),
        name="dsa_project",
    )(x2d, w_pad, c128, s128, c64, s64)


def _mono_key(s):
    i = lax.bitcast_convert_type(s, i32)
    return i ^ ((i >> 31) & jnp.int32(INT_MAX))


def _dsa_body(q_ref, qi_ref, wi_ref, x_ref, k_hbm, v_hbm, ki_hbm, wo_ref, g_ref, b_ref,
              o_ref, k_vm, v_vm, ki_vm, keys_ref, m_ref, l_ref, acc_ref, sem,
              *, tq, kb_size, nkb_total, q_pos0, l_real, topk):
    bi = pl.program_id(0)
    qi_blk = pl.program_id(1)

    @pl.when(qi_blk == 0)
    def _():
        copies = [pltpu.make_async_copy(k_hbm.at[bi], k_vm, sem.at[0]),
                  pltpu.make_async_copy(v_hbm.at[bi], v_vm, sem.at[1]),
                  pltpu.make_async_copy(ki_hbm.at[bi], ki_vm, sem.at[2])]
        for c in copies:
            c.start()
        for c in copies:
            c.wait()

    q0 = q_pos0 + qi_blk * tq
    last_chunk_end = ((q0 + tq - 1) // CHUNK + 1) * CHUNK
    nkb = jnp.minimum((last_chunk_end + kb_size - 1) // kb_size, nkb_total)

    qpos = q0 + lax.broadcasted_iota(i32, (tq, 1), 0)
    qchunk = qpos >> 6
    lane_k = lax.broadcasted_iota(i32, (1, kb_size), 1)

    def valid_mask(kb):
        kpos = kb * kb_size + lane_k
        ok = (kpos >> 6) <= qchunk
        return ok, kpos

    qi_all = qi_ref[...].reshape(IDX_HEADS * tq, LANES)
    wi = wi_ref[...]
    wcols = [wi[:, h:h + 1] for h in range(IDX_HEADS)]

    def score_block(kb, carry):
        off = pl.multiple_of(kb * kb_size, kb_size)
        kiblk = ki_vm[pl.ds(off, kb_size), :]
        rel = _dg(qi_all, kiblk, ((1,), (1,))).reshape(IDX_HEADS, tq, kb_size)
        score = jnp.maximum(rel[0], 0.0) * wcols[0]
        for h in range(1, IDX_HEADS):
            score = score + jnp.maximum(rel[h], 0.0) * wcols[h]
        adm, kpos = valid_mask(kb)
        key = _mono_key(jnp.where(adm, score, NEG_INF))
        if l_real < nkb_total * kb_size:
            key = jnp.where(kpos < l_real, key, jnp.int32(INT_MIN))
        keys_ref[kb] = key
        return carry

    lax.fori_loop(0, nkb, score_block, 0)

    nchunk = kb_size // LANES

    def count_where(pred):
        def body(kb, accv):
            for c in range(nchunk):
                blk = keys_ref[kb, :, c * LANES:(c + 1) * LANES]
                accv = accv + pred(blk, kb * kb_size + c * LANES).astype(i32)
            return accv
        accv = lax.fori_loop(0, nkb, body, jnp.zeros((tq, LANES), i32))
        return jnp.sum(accv, axis=1, keepdims=True)

    def bit_step(it, p):
        bit = 31 - it
        cand = p | (jnp.int32(1) << bit)
        thr = jnp.broadcast_to(cand ^ jnp.int32(INT_MIN), (tq, LANES))
        cnt = count_where(lambda blk, _: blk >= thr)
        return jnp.where(cnt >= topk, cand, p)

    p_fin = lax.fori_loop(0, 32, bit_step, jnp.zeros((tq, 1), i32))
    thr = p_fin ^ jnp.int32(INT_MIN)
    thr_b = jnp.broadcast_to(thr, (tq, LANES))
    cnt_ge = count_where(lambda blk, _: blk >= thr_b)
    cnt_gt = count_where(lambda blk, _: blk > thr_b)
    need = topk - cnt_gt
    has_excess = cnt_ge > topk
    lane_i = lax.broadcasted_iota(i32, (tq, LANES), 1)

    def tie_limit():
        def idx_step(it, jj):
            bit = 14 - it
            cand = jj | (jnp.int32(1) << bit)
            cand_b = jnp.broadcast_to(cand, (tq, LANES))
            g = count_where(lambda blk, base: (blk == thr_b) & ((lane_i + base) < cand_b))
            return jnp.where(g < need, cand, jj)
        jj = lax.fori_loop(0, 15, idx_step, jnp.zeros((tq, 1), i32))
        return jnp.where(has_excess, jj, jnp.int32(INT_MAX))

    any_excess = jnp.max(has_excess.astype(f32)) > 0.0
    tie_j = lax.cond(any_excess, tie_limit, lambda: jnp.full((tq, 1), INT_MAX, i32))

    m_ref[...] = jnp.full(m_ref.shape, NEG_INF, f32)
    l_ref[...] = jnp.zeros(l_ref.shape, f32)
    acc_ref[...] = jnp.zeros(acc_ref.shape, f32)
    gq = A_GROUP * tq

    def attn_block(kb, carry):
        off = pl.multiple_of(kb * kb_size, kb_size)
        key = keys_ref[kb]
        adm, kpos = valid_mask(kb)
        sel = (key > thr) | ((key == thr) & (kpos <= tie_j))
        mask = sel & adm
        if l_real < nkb_total * kb_size:
            mask = mask & (kpos < l_real)
        for g in range(A_KV_HEADS):
            qg = q_ref[g * A_GROUP:(g + 1) * A_GROUP].reshape(gq, LANES)
            kblk = k_vm[pl.ds(off, kb_size), g * LANES:(g + 1) * LANES]
            vblk = v_vm[pl.ds(off, kb_size), g * LANES:(g + 1) * LANES]
            s = _dg(qg, kblk, ((1,), (1,))).reshape(A_GROUP, tq, kb_size)
            s = jnp.where(mask[None], s, NEG_INF).reshape(gq, kb_size)
            m_old = m_ref[g]
            m_new = jnp.maximum(m_old, jnp.max(s, axis=1, keepdims=True))
            alpha = jnp.exp(m_old - m_new)
            p = jnp.exp(s - m_new)
            l_ref[g] = alpha * l_ref[g] + jnp.sum(p, axis=1, keepdims=True)
            acc_ref[g] = alpha * acc_ref[g] + jnp.dot(p.astype(bf16), vblk, preferred_element_type=f32)
            m_ref[g] = m_new
        return carry

    lax.fori_loop(0, nkb, attn_block, 0)

    heads = []
    for g in range(A_KV_HEADS):
        og = acc_ref[g] / l_ref[g]
        for j in range(A_GROUP):
            heads.append(og[j * tq:(j + 1) * tq])
    o = jnp.concatenate(heads, axis=1)
    y = jnp.dot(o.astype(bf16), wo_ref[...], preferred_element_type=f32)
    o_ref[...] = _res_ln(x_ref[...], y, g_ref[...], b_ref[...])


def _dsa_attend(q_hm, qi_hm, wi, x2d, kb_all, vb_all, kib_all, w_out, g, b,
                *, n_batch, tq, kb_size, q_pos0, l_real, topk):
    R = x2d.shape[0]
    L = kb_all.shape[1]
    nq = R // (n_batch * tq)
    nkb_total = L // kb_size
    body = functools.partial(_dsa_body, tq=tq, kb_size=kb_size, nkb_total=nkb_total,
                             q_pos0=q_pos0, l_real=l_real, topk=topk)
    hm = pl.BlockSpec((A_HEADS, tq, LANES), lambda bi, qi: (0, bi * nq + qi, 0))
    row = lambda w: pl.BlockSpec((tq, w), lambda bi, qi: (bi * nq + qi, 0))
    anyspec = pl.BlockSpec(memory_space=pl.ANY)
    return pl.pallas_call(
        body,
        grid=(n_batch, nq),
        in_specs=[hm, hm, row(LANES), row(D_MODEL), anyspec, anyspec, anyspec,
                  _const_spec((A_QW, D_MODEL)), _const_spec((1, D_MODEL)), _const_spec((1, D_MODEL))],
        out_specs=row(D_MODEL),
        out_shape=jax.ShapeDtypeStruct((R, D_MODEL), f32),
        scratch_shapes=[
            pltpu.VMEM((L, A_KW), bf16),
            pltpu.VMEM((L, A_KW), bf16),
            pltpu.VMEM((L, LANES), bf16),
            pltpu.VMEM((nkb_total, tq, kb_size), i32),
            pltpu.VMEM((A_KV_HEADS, A_GROUP * tq, 1), f32),
            pltpu.VMEM((A_KV_HEADS, A_GROUP * tq, 1), f32),
            pltpu.VMEM((A_KV_HEADS, A_GROUP * tq, LANES), f32),
            pltpu.SemaphoreType.DMA((3,)),
        ],
        compiler_params=_cparams(("arbitrary", "arbitrary")),
        name="dsa_attend",
    )(q_hm, qi_hm, wi, x2d, kb_all, vb_all, kib_all, w_out, g, b)


def _rglru_body(x_ref, h0_ref, cbuf_ref, win_ref, cw_ref, cb_ref, wa_ref, ba_ref, wx_ref, bx_ref,
                lam_ref, wout_ref, g_ref, b_ref,
                o_ref, hlast_ref, nbuf_ref,
                cconv_ref, hcar_ref, a_s, b_s, h_s, *, tb):
    t = pl.program_id(1)
    W = RNN_WIDTH

    @pl.when(t == 0)
    def _():
        cconv_ref[5:8, :] = cbuf_ref[0]
        hcar_ref[...] = h0_ref[0]

    x = x_ref[0]
    z = jnp.dot(x.astype(bf16), win_ref[...], preferred_element_type=f32)
    gate = z[:, :W]
    rec = z[:, W:]
    c3, c2, c1 = cconv_ref[5:6, :], cconv_ref[6:7, :], cconv_ref[7:8, :]
    r1, r2, r3 = _shift_rows(rec, [c3, c2, c1])
    u = (cw_ref[0:1, :] * r3 + cw_ref[1:2, :] * r2 + cw_ref[2:3, :] * r1 + cw_ref[3:4, :] * rec + cb_ref[...])
    cconv_ref[5:8, :] = rec[tb - 3:tb, :]
    nbuf_ref[0] = rec[tb - 3:tb, :]

    ub = u.astype(bf16)
    r = jax.nn.sigmoid(jnp.dot(ub, wa_ref[...], preferred_element_type=f32) + ba_ref[...])
    ig = jax.nn.sigmoid(jnp.dot(ub, wx_ref[...], preferred_element_type=f32) + bx_ref[...])
    log_a = (-LRU_C) * r * jax.nn.softplus(-lam_ref[...])
    a = jnp.exp(log_a)
    one_m_a2 = -jnp.tanh(log_a) * (a * a + 1.0)
    bv = jnp.sqrt(one_m_a2) * (ig * u)

    ng = tb // SUBLANES
    a3 = a.reshape(ng, SUBLANES, W)
    b3 = bv.reshape(ng, SUBLANES, W)
    sub = lax.broadcasted_iota(i32, (ng, SUBLANES, W), 1)
    off = 1
    while off < SUBLANES:
        m = sub >= off
        b3 = jnp.where(m, a3 * pltpu.roll(b3, off, axis=1) + b3, b3)
        a3 = jnp.where(m, a3 * pltpu.roll(a3, off, axis=1), a3)
        off *= 2
    a_s[...] = a3.reshape(tb, W)
    b_s[...] = b3.reshape(tb, W)

    def grp(gi, carry):
        r0 = pl.multiple_of(gi * SUBLANES, SUBLANES)
        hg = a_s[pl.ds(r0, SUBLANES), :] * carry + b_s[pl.ds(r0, SUBLANES), :]
        h_s[pl.ds(r0, SUBLANES), :] = hg
        return hg[SUBLANES - 1:SUBLANES, :]

    carry = lax.fori_loop(0, ng, grp, hcar_ref[...])
    hcar_ref[...] = carry
    hlast_ref[0] = carry

    y = jax.nn.gelu(gate) * h_s[...]
    out = jnp.dot(y.astype(bf16), wout_ref[...], preferred_element_type=f32)
    o_ref[0] = _res_ln(x, out, g_ref[...], b_ref[...])


def _mixer_b(x, h0, cbuf, w_in, conv_w, conv_b, wa, ba, wx, bx, lam, w_out, g, b, tb):
    B, T, _ = x.shape
    W = RNN_WIDTH
    nt = T // tb
    per_b = lambda r, c: pl.BlockSpec((1, r, c), lambda bi, ti: (bi, 0, 0))
    return pl.pallas_call(
        functools.partial(_rglru_body, tb=tb),
        grid=(B, nt),
        in_specs=[
            pl.BlockSpec((1, tb, D_MODEL), lambda bi, ti: (bi, ti, 0)),
            per_b(1, W), per_b(B_CONV - 1, W),
            _const_spec((D_MODEL, 2 * W)), _const_spec((B_CONV, W)), _const_spec((1, W)),
            _const_spec((W, W)), _const_spec((1, W)), _const_spec((W, W)), _const_spec((1, W)),
            _const_spec((1, W)), _const_spec((W, D_MODEL)), _const_spec((1, D_MODEL)), _const_spec((1, D_MODEL)),
        ],
        out_specs=[
            pl.BlockSpec((1, tb, D_MODEL), lambda bi, ti: (bi, ti, 0)),
            per_b(1, W), per_b(B_CONV - 1, W),
        ],
        out_shape=[jax.ShapeDtypeStruct((B, T, D_MODEL), f32),
                   jax.ShapeDtypeStruct((B, 1, W), f32),
                   jax.ShapeDtypeStruct((B, B_CONV - 1, W), f32)],
        scratch_shapes=[pltpu.VMEM((SUBLANES, W), f32), pltpu.VMEM((1, W), f32),
                        pltpu.VMEM((tb, W), f32), pltpu.VMEM((tb, W), f32), pltpu.VMEM((tb, W), f32)],
        compiler_params=_cparams(("arbitrary", "arbitrary")),
        name="rglru_mixer",
    )(x, h0, cbuf, w_in, conv_w, conv_b, wa, ba, wx, bx, lam, w_out, g, b)


C_HALO = 32


def _conformer_body(x_ref, cbuf_ref, win_ref, bin_ref, cw_ref, cb_ref, lg_ref, lb_ref, wout_ref, bout_ref,
                    g_ref, b_ref, o_ref, nbuf_ref, ext_ref, conv_ref, *, tb):
    t = pl.program_id(1)
    C = C_WIDTH
    hist = C_CONV - 1

    @pl.when(t == 0)
    def _():
        ext_ref[0:C_HALO - hist, :] = jnp.zeros((C_HALO - hist, C), f32)
        ext_ref[C_HALO - hist:C_HALO, :] = cbuf_ref[0]

    x = x_ref[0]
    z = jnp.dot(x.astype(bf16), win_ref[...], preferred_element_type=f32) + bin_ref[...]
    u = z[:, :C] * jax.nn.sigmoid(z[:, C:])
    ext_ref[C_HALO:C_HALO + tb, :] = u

    rt = min(tb, 128)
    for r0 in range(0, tb, rt):
        for c0 in range(0, C, LANES):
            acc = jnp.broadcast_to(cb_ref[:, c0:c0 + LANES], (rt, LANES))
            for k in range(C_CONV):
                acc = acc + cw_ref[k:k + 1, c0:c0 + LANES] * ext_ref[pl.ds(r0 + k + C_HALO - hist, rt), c0:c0 + LANES]
            conv_ref[r0:r0 + rt, c0:c0 + LANES] = acc

    nbuf_ref[0] = ext_ref[tb + C_HALO - hist:tb + C_HALO, :]
    ext_ref[0:C_HALO, :] = ext_ref[tb:tb + C_HALO, :]

    c = _layer_norm(conv_ref[...], lg_ref[...], lb_ref[...])
    s = jax.nn.silu(c)
    out = jnp.dot(s.astype(bf16), wout_ref[...], preferred_element_type=f32) + bout_ref[...]
    o_ref[0] = _res_ln(x, out, g_ref[...], b_ref[...])


def _mixer_c(x, cbuf, w_in, b_in, conv_w, conv_b, ln_g, ln_b, w_out, b_out, g, b, tb):
    B, T, _ = x.shape
    C = C_WIDTH
    nt = T // tb
    return pl.pallas_call(
        functools.partial(_conformer_body, tb=tb),
        grid=(B, nt),
        in_specs=[
            pl.BlockSpec((1, tb, D_MODEL), lambda bi, ti: (bi, ti, 0)),
            pl.BlockSpec((1, C_CONV - 1, C), lambda bi, ti: (bi, 0, 0)),
            _const_spec((D_MODEL, 2 * C)), _const_spec((1, 2 * C)),
            _const_spec((C_CONV, C)), _const_spec((1, C)), _const_spec((1, C)), _const_spec((1, C)),
            _const_spec((C, D_MODEL)), _const_spec((1, D_MODEL)),
            _const_spec((1, D_MODEL)), _const_spec((1, D_MODEL)),
        ],
        out_specs=[
            pl.BlockSpec((1, tb, D_MODEL), lambda bi, ti: (bi, ti, 0)),
            pl.BlockSpec((1, C_CONV - 1, C), lambda bi, ti: (bi, 0, 0)),
        ],
        out_shape=[jax.ShapeDtypeStruct((B, T, D_MODEL), f32),
                   jax.ShapeDtypeStruct((B, C_CONV - 1, C), f32)],
        scratch_shapes=[pltpu.VMEM((tb + C_HALO, C), f32), pltpu.VMEM((tb, C), f32)],
        compiler_params=_cparams(("arbitrary", "arbitrary")),
        name="conformer_mixer",
    )(x, cbuf, w_in, b_in, conv_w, conv_b, ln_g, ln_b, w_out, b_out, g, b)


def _rwkv_front_body(x_ref, sh_ref, mu_ref, wr_ref, wk_ref, wv_ref, w0_ref, w1_ref, w2_ref,
                     a0_ref, a1_ref, a2_ref, g1_ref, g2_ref, kk_ref, ka_ref, mseg_ref,
                     r_o, lw_o, k_o, v_o, an_o, bk_o, g_o, shout_ref, car_ref, *, tb):
    t = pl.program_id(1)

    @pl.when(t == 0)
    def _():
        car_ref[...] = sh_ref[0]

    x = x_ref[0]
    (prev,) = _shift_rows(x, [car_ref[...]])
    car_ref[...] = x[tb - 1:tb, :]
    shout_ref[0] = x[tb - 1:tb, :]
    xx = prev - x
    xr, xw, xk, xv, xa, xg = (x + xx * mu_ref[n:n + 1, :] for n in range(6))
    r = _mm(xr, wr_ref[...])
    k = _mm(xk, wk_ref[...])
    v = _mm(xv, wv_ref[...])
    wl = w0_ref[...] + _mm(jnp.tanh(_mm(xw, w1_ref[...])), w2_ref[...])
    w_log = -jax.nn.softplus(-wl) - 0.5
    lw = -jnp.exp(w_log)
    a = jax.nn.sigmoid(a0_ref[...] + _mm(_mm(xa, a1_ref[...]), a2_ref[...]))
    g = _mm(jax.nn.sigmoid(_mm(xg, g1_ref[...])), g2_ref[...])
    kk = k * kk_ref[...]
    n2 = _dot2(kk * kk, mseg_ref[...])
    kk = kk / jnp.maximum(jnp.sqrt(n2), 1e-12)
    k2 = k * (1.0 + (a - 1.0) * ka_ref[...])
    outs = ((r_o, r), (lw_o, lw), (k_o, k2), (v_o, v), (an_o, -kk), (bk_o, kk * a), (g_o, g))
    for ref, val in outs:
        for p in range(D_PAIRS):
            ref[0, p] = val[:, p * LANES:(p + 1) * LANES]


def _rwkv_front(x, shift0, mu, w_r, w_k, w_v, w0, w1, w2, a0, a1, a2, g1, g2, k_k, k_a, mseg, tb):
    B, T, D = x.shape
    nt = T // tb
    pm = pl.BlockSpec((1, D_PAIRS, tb, LANES), lambda bi, ti: (bi, 0, ti, 0))
    pm_shape = jax.ShapeDtypeStruct((B, D_PAIRS, T, LANES), f32)
    vec = _const_spec((1, D))
    return pl.pallas_call(
        functools.partial(_rwkv_front_body, tb=tb),
        grid=(B, nt),
        in_specs=[
            pl.BlockSpec((1, tb, D), lambda bi, ti: (bi, ti, 0)),
            pl.BlockSpec((1, 1, D), lambda bi, ti: (bi, 0, 0)),
            _const_spec((6, D)), _const_spec((D, D)), _const_spec((D, D)), _const_spec((D, D)),
            vec, _const_spec((D, LORA_PAD)), _const_spec((LORA_PAD, D)),
            vec, _const_spec((D, LORA_PAD)), _const_spec((LORA_PAD, D)),
            _const_spec((D, LORA_PAD)), _const_spec((LORA_PAD, D)),
            vec, vec, _const_spec((D, D)),
        ],
        out_specs=[pm] * 7 + [pl.BlockSpec((1, 1, D), lambda bi, ti: (bi, 0, 0))],
        out_shape=[pm_shape] * 7 + [jax.ShapeDtypeStruct((B, 1, D), f32)],
        scratch_shapes=[pltpu.VMEM((1, D), f32)],
        compiler_params=_cparams(("arbitrary", "arbitrary")),
        name="rwkv_front",
    )(x, shift0, mu, w_r, w_k, w_v, w0, w1, w2, a0, a1, a2, g1, g2, k_k, k_a, mseg)


def _wkv_body(r_ref, lw_ref, k_ref, v_ref, an_ref, bk_ref, g_ref, s0_ref, rk_ref, gng_ref, gnb_ref,
              y_ref, sout_ref, s_ref, *, tb):
    t = pl.program_id(2)
    C = WKV_CHUNK
    P2 = 2 * C

    @pl.when(t == 0)
    def _():
        s_ref[...] = s0_ref[0, 0]

    lane = lax.broadcasted_iota(i32, (1, LANES), 1)
    m0 = (lane < D_HEAD_DIM).astype(f32)
    m1 = 1.0 - m0
    ri = lax.broadcasted_iota(i32, (P2, P2), 0)
    ci = lax.broadcasted_iota(i32, (P2, P2), 1)
    same_head = (ri >= C) == (ci >= C)
    s_idx = ri & (C - 1)
    t_idx = ci & (C - 1)
    mask_strict = same_head & (s_idx < t_idx)
    mask_incl = same_head & (s_idx <= t_idx)
    eye = (ri == ci).astype(f32)
    seg_ones = ((ri >= D_HEAD_DIM) == (ci >= D_HEAD_DIM)).astype(f32)
    tr = lax.broadcasted_iota(i32, (C, C), 0)
    tc = lax.broadcasted_iota(i32, (C, C), 1)
    tri_incl = (tc <= tr).astype(f32)

    def stack2(z):
        return jnp.concatenate([z * m0, z * m1], axis=0)

    dims_nt = ((1,), (1,))
    dims_tn = ((0,), (0,))
    ys = []
    for c in range(tb // C):
        sl = slice(c * C, (c + 1) * C)
        r, lw, k, v = r_ref[0, 0, sl, :], lw_ref[0, 0, sl, :], k_ref[0, 0, sl, :], v_ref[0, 0, sl, :]
        an, bk = an_ref[0, 0, sl, :], bk_ref[0, 0, sl, :]
        cs = _dot2l(tri_incl, lw)
        cs_end = cs[C - 1:C, :]
        p_in = jnp.exp(cs)
        p_inv = jnp.exp(-cs)
        p_prev = jnp.exp(cs - lw)
        p_tail = jnp.exp(cs_end - cs)
        ah = an * p_prev
        bh = bk * p_inv
        kh = k * p_inv
        rh = r * p_in
        S = s_ref[...]
        X = jnp.concatenate([stack2(bh), stack2(kh)], axis=0)
        Yr = jnp.concatenate([ah, ah, rh, rh], axis=0)
        G = _dot3(X, Yr, dims_nt)
        n_ab = jnp.where(mask_strict, G[0:P2, 0:P2], 0.0)
        n_ak = jnp.where(mask_strict, G[P2:2 * P2, 0:P2], 0.0)
        n_rb = jnp.where(mask_incl, G[0:P2, P2:2 * P2], 0.0)
        n_rk = jnp.where(mask_incl, G[P2:2 * P2, P2:2 * P2], 0.0)
        tt = eye + n_ab
        mpow = n_ab
        steps = 1
        while steps * 2 < C:
            mpow = _dot3(mpow, mpow)
            tt = tt + _dot3(tt, mpow)
            steps *= 2
        a2, r2, v2 = stack2(ah), stack2(rh), stack2(v)
        ars = _dot3(jnp.concatenate([a2, r2], axis=0), S, dims_nt)
        rhs = ars[0:P2] + _dot3(n_ak, v2, dims_tn)
        u2 = _dot3(tt, rhs, dims_tn)
        y2 = ars[P2:2 * P2] + _dot3(n_rb, u2, dims_tn) + _dot3(n_rk, v2, dims_tn)
        ys.append(y2[0:C] + y2[C:P2])
        bt2 = stack2(bk * p_tail)
        kt2 = stack2(k * p_tail)
        s_ref[...] = S * jnp.exp(cs_end) + _dot3(u2, bt2, dims_tn) + _dot3(v2, kt2, dims_tn)

    sout_ref[0, 0] = s_ref[...]
    y = ys[0] if len(ys) == 1 else jnp.concatenate(ys, axis=0)
    r, k, v = r_ref[0, 0], k_ref[0, 0], v_ref[0, 0]
    inv = 1.0 / D_HEAD_DIM
    mu_y = _dot3(y, seg_ones) * inv
    d = y - mu_y
    var = _dot3(d * d, seg_ones) * inv
    yn = d * lax.rsqrt(var + GN_EPS) * gng_ref[0] + gnb_ref[0]
    bonus = _dot3(r * k * rk_ref[0], seg_ones) * v
    y_ref[0, 0] = (yn + bonus) * g_ref[0, 0]


def _wkv(r, lw, k, v, an, bk, g, s0, r_k, gn_g, gn_b, tb):
    B, NP, T, _ = r.shape
    nt = T // tb
    pm = pl.BlockSpec((1, 1, tb, LANES), lambda bi, pi, ti: (bi, pi, ti, 0))
    st = pl.BlockSpec((1, 1, LANES, LANES), lambda bi, pi, ti: (bi, pi, 0, 0))
    par = pl.BlockSpec((1, 1, LANES), lambda bi, pi, ti: (pi, 0, 0))
    return pl.pallas_call(
        functools.partial(_wkv_body, tb=tb),
        grid=(B, NP, nt),
        in_specs=[pm] * 7 + [st, par, par, par],
        out_specs=[pm, st],
        out_shape=[jax.ShapeDtypeStruct((B, NP, T, LANES), f32),
                   jax.ShapeDtypeStruct((B, NP, LANES, LANES), f32)],
        scratch_shapes=[pltpu.VMEM((LANES, LANES), f32)],
        compiler_params=_cparams(("arbitrary", "arbitrary", "arbitrary")),
        name="rwkv_wkv",
    )(r, lw, k, v, an, bk, g, s0, r_k, gn_g, gn_b)


def _rwkv_out_body(y_ref, x_ref, wo_ref, g_ref, b_ref, o_ref):
    y = jnp.concatenate([y_ref[0, p] for p in range(D_PAIRS)], axis=1)
    out = jnp.dot(y.astype(bf16), wo_ref[...], preferred_element_type=f32)
    o_ref[0] = _res_ln(x_ref[0], out, g_ref[...], b_ref[...])


def _rwkv_out(y_pm, x, w_o, g, b, tb):
    B, T, D = x.shape
    nt = T // tb
    return pl.pallas_call(
        _rwkv_out_body,
        grid=(B, nt),
        in_specs=[pl.BlockSpec((1, D_PAIRS, tb, LANES), lambda bi, ti: (bi, 0, ti, 0)),
                  pl.BlockSpec((1, tb, D), lambda bi, ti: (bi, ti, 0)),
                  _const_spec((D, D)), _const_spec((1, D)), _const_spec((1, D))],
        out_specs=pl.BlockSpec((1, tb, D), lambda bi, ti: (bi, ti, 0)),
        out_shape=jax.ShapeDtypeStruct((B, T, D), f32),
        compiler_params=_cparams(("arbitrary", "arbitrary")),
        name="rwkv_out",
    )(y_pm, x, w_o, g, b)


def _rope_tables(pos, head_dim, reps):
    half = head_dim // 2
    inv_freq = jnp.exp(-math.log(ROPE_THETA) * jnp.arange(half, dtype=f32) / half)
    ang = pos.astype(f32)[:, None] * inv_freq[None, :]
    cos, sin = jnp.cos(ang), jnp.sin(ang)
    cos_t = jnp.tile(jnp.concatenate([cos, cos], axis=1), (1, reps))
    sin_t = jnp.tile(jnp.concatenate([-sin, sin], axis=1), (1, reps))
    return cos_t, sin_t


def _pad_a_w_in(w):
    q = w[:, :A_QW]
    k = w[:, A_QW:A_QW + A_KW]
    v = w[:, A_QW + A_KW:A_QW + 2 * A_KW]
    o = A_QW + 2 * A_KW
    qi = w[:, o:o + IDX_HEADS * IDX_DIM].reshape(D_MODEL, IDX_HEADS, IDX_DIM)
    qi = jnp.pad(qi, ((0, 0), (0, 0), (0, LANES - IDX_DIM))).reshape(D_MODEL, IDX_HEADS * LANES)
    o += IDX_HEADS * IDX_DIM
    ki = jnp.pad(w[:, o:o + IDX_DIM], ((0, 0), (0, LANES - IDX_DIM)))
    o += IDX_DIM
    wi = jnp.pad(w[:, o:o + IDX_HEADS], ((0, 0), (0, LANES - IDX_HEADS)))
    return jnp.concatenate([q, k, v, qi, ki, wi], axis=1).astype(bf16)


def _block_diag(w):
    n, d, _ = w.shape
    eye = jnp.eye(n, dtype=w.dtype)
    return (eye[:, None, :, None] * w[:, :, None, :]).reshape(n * d, n * d)


def _pair_state(s):
    B = s.shape[0]
    s = s.reshape(B, D_PAIRS, 2, D_HEAD_DIM, D_HEAD_DIM)
    z = jnp.zeros_like(s[:, :, 0])
    top = jnp.concatenate([s[:, :, 0], z], axis=-1)
    bot = jnp.concatenate([z, s[:, :, 1]], axis=-1)
    return jnp.concatenate([top, bot], axis=-2)


def _unpair_state(sp):
    B = sp.shape[0]
    h0 = sp[:, :, :D_HEAD_DIM, :D_HEAD_DIM]
    h1 = sp[:, :, D_HEAD_DIM:, D_HEAD_DIM:]
    return jnp.stack([h0, h1], axis=2).reshape(B, D_HEADS, D_HEAD_DIM, D_HEAD_DIM)


def _row(v):
    return v.reshape(1, -1).astype(f32)


def _mixer_a_stream(x, w_pad, w_out, g, b, pos, cache, *, tb, tq, kb_size):
    B, T, D = x.shape
    R = B * T
    c128, s128 = _rope_tables(pos, A_HEAD_DIM, 1)
    c64, s64 = _rope_tables(pos, IDX_DIM, 2)
    if tb > T:
        rep = tb // T
        c128, s128, c64, s64 = (jnp.tile(a, (rep, 1)) for a in (c128, s128, c64, s64))
    x2d = x.reshape(R, D)
    q_hm, k_o, v_o, kb, vb, qi_hm, ki_o, kib, wi = _a_project(x2d, w_pad, c128, s128, c64, s64, tb)
    kb3, vb3, kib3 = kb.reshape(B, T, A_KW), vb.reshape(B, T, A_KW), kib.reshape(B, T, LANES)
    if cache is None:
        q_pos0, l_real = 0, T
    else:
        ck, cv, cki = cache
        P = ck.shape[1]
        kb3 = jnp.concatenate([ck.reshape(B, P, A_KW).astype(bf16), kb3], axis=1)
        vb3 = jnp.concatenate([cv.reshape(B, P, A_KW).astype(bf16), vb3], axis=1)
        cki_p = jnp.pad(cki, ((0, 0), (0, 0), (0, LANES - IDX_DIM))).astype(bf16)
        kib3 = jnp.concatenate([cki_p, kib3], axis=1)
        q_pos0, l_real = P, P + T
    l_pad = -(-l_real // kb_size) * kb_size
    if l_pad > l_real:
        padk = ((0, 0), (0, l_pad - l_real), (0, 0))
        kb3, vb3, kib3 = jnp.pad(kb3, padk), jnp.pad(vb3, padk), jnp.pad(kib3, padk)
    topk = min(TOPK_MAX, l_real // 4)
    xn = _dsa_attend(q_hm, qi_hm, wi, x2d, kb3, vb3, kib3, w_out, g, b,
                     n_batch=B, tq=tq, kb_size=kb_size, q_pos0=q_pos0, l_real=l_real, topk=topk)
    return (xn.reshape(B, T, D), k_o.reshape(B, T, A_KV_HEADS, A_HEAD_DIM),
            v_o.reshape(B, T, A_KV_HEADS, A_HEAD_DIM), ki_o.reshape(B, T, IDX_DIM))


def _mixer_d_stream(x, wkv0, shift0, prm, g, b, *, tb, tb_wkv):
    B, T, D = x.shape
    outs = _rwkv_front(x, shift0.reshape(B, 1, D), prm["mu"], prm["w_r"], prm["w_k"], prm["w_v"], prm["w0"],
                       prm["w1"], prm["w2"], prm["a0"], prm["a1"], prm["a2"], prm["g1"], prm["g2"],
                       prm["k_k"], prm["k_a"], prm["mseg"], tb)
    seq, shift_out = outs[:7], outs[7]
    t_pad = -(-T // tb_wkv) * tb_wkv
    if t_pad > T:
        seq = [jnp.pad(a, ((0, 0), (0, 0), (0, t_pad - T), (0, 0))) for a in seq]
    y_pm, s_out = _wkv(*seq, _pair_state(wkv0), prm["r_k"], prm["gn_g"], prm["gn_b"], tb_wkv)
    if t_pad > T:
        y_pm = y_pm[:, :, :T]
    xn = _rwkv_out(y_pm, x, prm["w_o"], g, b, tb)
    return xn, _unpair_state(s_out), shift_out.reshape(B, D)


def kernel(x_prompt, x_sample, cache_a_k, cache_a_v, cache_a_kidx, state_b_h, state_b_conv, state_c_conv,
           state_d_wkv, state_d_shift, state_f_conv, ln_g, ln_b, a_w_in, a_w_out,
           b_w_in, b_conv_w, b_conv_b, b_gate_a_w, b_gate_a_b, b_gate_x_w, b_gate_x_b, b_lambda, b_w_out,
           c_w_in, c_b_in, c_conv_w, c_conv_b, c_ln_g, c_ln_b, c_w_out, c_b_out,
           d_mu, d_w_r, d_w_k, d_w_v, d_w_o, d_w0, d_w1, d_w2, d_a0, d_a1, d_a2, d_g1, d_g2,
           d_k_k, d_k_a, d_r_k, d_gn_g, d_gn_b, f_w_up, f_conv_w, f_conv_b, f_w_down):
    xp, xs = x_prompt, x_sample
    BP, TP, D = xp.shape
    BS, TS, _ = xs.shape
    tb_p = 256
    tb_s = TS
    outs = {k: [] for k in ("ak_p", "ak_s", "av_p", "av_s", "aki_p", "aki_s", "bh_p", "bh_s", "bc_p", "bc_s",
                            "cc_p", "cc_s", "dw_p", "dw_s", "dsh_p", "dsh_s", "fc_p", "fc_s")}
    for i in range(DEPTH):
        m, j = i % 4, i // 4
        g0, b0 = _row(ln_g[i, 0]), _row(ln_b[i, 0])
        g1, b1 = _row(ln_g[i, 1]), _row(ln_b[i, 1])
        if m == 0:
            w_pad = _pad_a_w_in(a_w_in[j])
            w_out = a_w_out[j].astype(bf16)
            xp, kp, vp, kip = _mixer_a_stream(xp, w_pad, w_out, g0, b0, jnp.arange(TP, dtype=i32), None,
                                              tb=512, tq=128, kb_size=512)
            P = cache_a_k.shape[2]
            xs, ks, vs, kis = _mixer_a_stream(xs, w_pad, w_out, g0, b0, P + jnp.arange(TS, dtype=i32),
                                              (cache_a_k[j], cache_a_v[j], cache_a_kidx[j]),
                                              tb=BS * TS, tq=TS, kb_size=256)
            outs["ak_p"].append(kp); outs["ak_s"].append(ks)
            outs["av_p"].append(vp); outs["av_s"].append(vs)
            outs["aki_p"].append(kip); outs["aki_s"].append(kis)
        elif m == 1:
            W = RNN_WIDTH
            args = (b_w_in[j].astype(bf16), b_conv_w[j], _row(b_conv_b[j]),
                    _block_diag(b_gate_a_w[j]).astype(bf16), _row(b_gate_a_b[j]),
                    _block_diag(b_gate_x_w[j]).astype(bf16), _row(b_gate_x_b[j]),
                    _row(b_lambda[j]), b_w_out[j].astype(bf16), g0, b0)
            xp, hp, cp = _mixer_b(xp, jnp.zeros((BP, 1, W), f32), jnp.zeros((BP, B_CONV - 1, W), f32), *args, tb_p)
            xs, hs, cs = _mixer_b(xs, state_b_h[j].reshape(BS, 1, W), state_b_conv[j], *args, tb_s)
            outs["bh_p"].append(hp.reshape(BP, W)); outs["bh_s"].append(hs.reshape(BS, W))
            outs["bc_p"].append(cp); outs["bc_s"].append(cs)
        elif m == 2:
            args = (c_w_in[j].astype(bf16), _row(c_b_in[j]), c_conv_w[j], _row(c_conv_b[j]),
                    _row(c_ln_g[j]), _row(c_ln_b[j]), c_w_out[j].astype(bf16), _row(c_b_out[j]), g0, b0)
            xp, cp = _mixer_c(xp, jnp.zeros((BP, C_CONV - 1, C_WIDTH), f32), *args, tb_p)
            xs, cs = _mixer_c(xs, state_c_conv[j], *args, tb_s)
            outs["cc_p"].append(cp); outs["cc_s"].append(cs)
        else:
            lp = ((0, 0), (0, LORA_PAD - d_w1.shape[2]))
            lq = ((0, LORA_PAD - d_w1.shape[2]), (0, 0))
            gp = ((0, 0), (0, LORA_PAD - d_g1.shape[2]))
            gq = ((0, LORA_PAD - d_g1.shape[2]), (0, 0))
            pairs = lambda v: v.reshape(D_PAIRS, 1, LANES).astype(f32)
            prm = dict(
                mu=d_mu[j], w_r=d_w_r[j].astype(bf16), w_k=d_w_k[j].astype(bf16), w_v=d_w_v[j].astype(bf16),
                w_o=d_w_o[j].astype(bf16), w0=_row(d_w0[j]),
                w1=jnp.pad(d_w1[j], lp).astype(bf16), w2=jnp.pad(d_w2[j], lq).astype(bf16),
                a0=_row(d_a0[j]), a1=jnp.pad(d_a1[j], lp).astype(bf16), a2=jnp.pad(d_a2[j], lq).astype(bf16),
                g1=jnp.pad(d_g1[j], gp).astype(bf16), g2=jnp.pad(d_g2[j], gq).astype(bf16),
                k_k=_row(d_k_k[j]), k_a=_row(d_k_a[j]),
                mseg=_block_diag(jnp.ones((D_HEADS, D_HEAD_DIM, D_HEAD_DIM), bf16)),
                r_k=pairs(d_r_k[j]), gn_g=pairs(d_gn_g[j]), gn_b=pairs(d_gn_b[j]))
            xp, sp, shp = _mixer_d_stream(xp, jnp.zeros((BP, D_HEADS, D_HEAD_DIM, D_HEAD_DIM), f32),
                                          jnp.zeros((BP, D), f32), prm, g0, b0, tb=tb_p, tb_wkv=256)
            xs, ss, shs = _mixer_d_stream(xs, state_d_wkv[j], state_d_shift[j], prm, g0, b0,
                                          tb=tb_s, tb_wkv=WKV_CHUNK)
            outs["dw_p"].append(sp); outs["dw_s"].append(ss)
            outs["dsh_p"].append(shp); outs["dsh_s"].append(shs)
        fargs = (f_w_up[i].astype(bf16), f_conv_w[i], _row(f_conv_b[i]), f_w_down[i].astype(bf16), g1, b1)
        xp, fbp = _conv_ffn(xp, jnp.zeros((BP, F_CONV - 1, 2 * D_FF), f32), *fargs, tb_p)
        xs, fbs = _conv_ffn(xs, state_f_conv[i], *fargs, tb_s)
        outs["fc_p"].append(fbp); outs["fc_s"].append(fbs)
    st = lambda k: jnp.stack(outs[k])
    return (xp, xs,
            st("ak_p"), st("ak_s"), st("av_p"), st("av_s"), st("aki_p"), st("aki_s"),
            st("bh_p"), st("bh_s"), st("bc_p"), st("bc_s"), st("cc_p"), st("cc_s"),
            st("dw_p"), st("dw_s"), st("dsh_p"), st("dsh_s"), st("fc_p"), st("fc_s"))
```

```python
import functools
import math

import jax
import jax.numpy as jnp
from jax import lax
from jax.experimental import pallas as pl
from jax.experimental.pallas import tpu as pltpu

f32 = jnp.float32
bf16 = jnp.bfloat16
i32 = jnp.int32

D_MODEL = 1024
DEPTH = 4
CHUNK = 64
ALPHA = (2.0 * DEPTH) ** 0.25
LN_EPS = 1e-5
NEG_INF = -1e30

A_HEADS = 8
A_KV_HEADS = 2
A_HEAD_DIM = 128
A_GROUP = A_HEADS // A_KV_HEADS
IDX_HEADS = 8
IDX_DIM = 64
IDX_SCALE = (IDX_DIM ** -0.5) * (IDX_HEADS ** -0.5)
TOPK_MAX = 256
ROPE_THETA = 10000.0

RNN_WIDTH = 1408
LRU_BLOCKS = 8
LRU_BLOCK_DIM = RNN_WIDTH // LRU_BLOCKS
B_CONV = 4
LRU_C = 8.0

C_WIDTH = 1024
C_CONV = 31

D_HEAD_DIM = 64
D_HEADS = D_MODEL // D_HEAD_DIM
D_PAIRS = D_HEADS // 2
GN_EPS = 64e-5
LORA_PAD = 128

D_FF = 2816
F_CONV = 3

LANES = 128
SUBLANES = 8
VMEM_LIMIT_BYTES = 60 * 1024 * 1024
WKV_CHUNK = 64

INT_MIN = -(2 ** 31)
INT_MAX = 2 ** 31 - 1


def _cparams(sem):
    return pltpu.CompilerParams(dimension_semantics=sem, vmem_limit_bytes=VMEM_LIMIT_BYTES)


def _const_spec(shape):
    nd = len(shape)
    return pl.BlockSpec(shape, lambda *_: (0,) * nd, pipeline_mode=pl.Buffered(1))


def _mm(a, b):
    return jnp.dot(a.astype(bf16), b.astype(bf16), preferred_element_type=f32)


def _split(a):
    hi = a.astype(bf16)
    lo = (a - hi.astype(f32)).astype(bf16)
    return hi, lo


def _dg(a, b, dims):
    return lax.dot_general(a, b, (dims, ((), ())), preferred_element_type=f32)


def _mm_nt(a, b):
    return _dg(a.astype(bf16), b.astype(bf16), ((1,), (1,)))


def _dot3(a, b, dims=((1,), (0,))):
    ah, al = _split(a)
    bh, bl = _split(b)
    return _dg(ah, bh, dims) + (_dg(ah, bl, dims) + _dg(al, bh, dims))


def _dot2(a, b_exact, dims=((1,), (0,))):
    ah, al = _split(a)
    bb = b_exact.astype(bf16)
    return _dg(ah, bb, dims) + _dg(al, bb, dims)


def _dot2l(a_exact, b, dims=((1,), (0,))):
    bh, bl = _split(b)
    aa = a_exact.astype(bf16)
    return _dg(aa, bh, dims) + _dg(aa, bl, dims)


def _layer_norm(z, g, b, eps=LN_EPS):
    mu = jnp.mean(z, axis=-1, keepdims=True)
    zc = z - mu
    var = jnp.mean(zc * zc, axis=-1, keepdims=True)
    return zc * lax.rsqrt(var + eps) * g + b


def _res_ln(x, y, g, b):
    return _layer_norm(ALPHA * x + y, g, b)


def _shift_rows(h, carry_rows):
    n = len(carry_rows)
    row = lax.broadcasted_iota(i32, (SUBLANES, h.shape[1]), 0)
    out = []
    for k in range(1, n + 1):
        hk = pltpu.roll(h, k, axis=0)
        top = hk[0:SUBLANES]
        for j in range(k):
            top = jnp.where(row == j, carry_rows[n - (k - j)], top)
        out.append(jnp.concatenate([top, hk[SUBLANES:]], axis=0) if h.shape[0] > SUBLANES else top)
    return out


FFN_TN = 256


def _ffn_body(x_ref, buf_ref, wup_ref, cw_ref, cb_ref, wdn_ref, g_ref, b_ref,
              o_ref, nbuf_ref, carry_ref, *, tb):
    t = pl.program_id(1)

    @pl.when(t == 0)
    def _():
        carry_ref[6:8, :] = buf_ref[0]

    x = x_ref[0]
    xb = x.astype(bf16)
    nj = D_FF // FFN_TN

    def up(j):
        return [jnp.dot(xb, wup_ref[:, part * D_FF + j * FFN_TN: part * D_FF + (j + 1) * FFN_TN],
                        preferred_element_type=f32) for part in range(2)]

    acc = jnp.zeros((tb, D_MODEL), f32)
    hs_next = up(0)
    for j in range(nj):
        hs = hs_next
        if j + 1 < nj:
            hs_next = up(j + 1)
        us = []
        for part in range(2):
            c0 = part * D_FF + j * FFN_TN
            h = hs[part]
            cm2 = carry_ref[6:7, c0:c0 + FFN_TN]
            cm1 = carry_ref[7:8, c0:c0 + FFN_TN]
            h1, h2 = _shift_rows(h, [cm2, cm1])
            u = (cw_ref[0:1, c0:c0 + FFN_TN] * h2 + cw_ref[1:2, c0:c0 + FFN_TN] * h1
                 + cw_ref[2:3, c0:c0 + FFN_TN] * h + cb_ref[:, c0:c0 + FFN_TN])
            carry_ref[6:8, c0:c0 + FFN_TN] = h[tb - 2:tb, :]
            us.append(u)
        act = jax.nn.silu(us[0]) * us[1]
        acc = acc + jnp.dot(act.astype(bf16), wdn_ref[j * FFN_TN:(j + 1) * FFN_TN, :],
                            preferred_element_type=f32)
    nbuf_ref[0] = carry_ref[6:8, :]
    o_ref[0] = _res_ln(x, acc, g_ref[...], b_ref[...])


def _conv_ffn(x, buf, w_up, conv_w, conv_b, w_down, g, b, tb):
    B, T, _ = x.shape
    nt = T // tb
    return pl.pallas_call(
        functools.partial(_ffn_body, tb=tb),
        grid=(B, nt),
        in_specs=[
            pl.BlockSpec((1, tb, D_MODEL), lambda bi, ti: (bi, ti, 0)),
            pl.BlockSpec((1, F_CONV - 1, 2 * D_FF), lambda bi, ti: (bi, 0, 0)),
            _const_spec((D_MODEL, 2 * D_FF)),
            _const_spec((F_CONV, 2 * D_FF)),
            _const_spec((1, 2 * D_FF)),
            _const_spec((D_FF, D_MODEL)),
            _const_spec((1, D_MODEL)),
            _const_spec((1, D_MODEL)),
        ],
        out_specs=[
            pl.BlockSpec((1, tb, D_MODEL), lambda bi, ti: (bi, ti, 0)),
            pl.BlockSpec((1, F_CONV - 1, 2 * D_FF), lambda bi, ti: (bi, 0, 0)),
        ],
        out_shape=[jax.ShapeDtypeStruct((B, T, D_MODEL), f32),
                   jax.ShapeDtypeStruct((B, F_CONV - 1, 2 * D_FF), f32)],
        scratch_shapes=[pltpu.VMEM((SUBLANES, 2 * D_FF), f32)],
        compiler_params=_cparams(("arbitrary", "arbitrary")),
        name="conv_ffn",
    )(x, buf, w_up, conv_w, conv_b, w_down, g, b)


A_QW = A_HEADS * A_HEAD_DIM
A_KW = A_KV_HEADS * A_HEAD_DIM
A_PROJ_PAD = A_QW + 2 * A_KW + IDX_HEADS * LANES + LANES + LANES
LOG2_E = math.log2(math.e)
ATTN_ROW_CHUNK = 32


def _rope128(z, cos, sin):
    return z * cos + pltpu.roll(z, A_HEAD_DIM // 2, axis=1) * sin


def _rope64(z, cos, sin):
    lane = lax.broadcasted_iota(i32, z.shape, 1)
    rot = jnp.where((lane & 32) == 0, pltpu.roll(z, LANES - 32, axis=1), pltpu.roll(z, 32, axis=1))
    return z * cos + rot * sin


def _aproj_body(x_ref, w_ref, c128_ref, s128_ref, c64_ref, s64_ref,
                q_ref, k_ref, v_ref, kb_ref, vb_ref, qi_ref, ki_ref, kib_ref, wi_ref):
    xb = x_ref[...].astype(bf16)
    c128, s128 = c128_ref[...], s128_ref[...]
    c64, s64 = c64_ref[...], s64_ref[...]
    qscale = A_HEAD_DIM ** -0.5 * LOG2_E

    def proj2(c0):
        z = jnp.dot(xb, w_ref[:, c0:c0 + 2 * LANES], preferred_element_type=f32)
        return z[:, :LANES], z[:, LANES:]

    for h in range(0, A_HEADS, 2):
        for hh, z in zip((h, h + 1), proj2(h * LANES)):
            q_ref[hh] = (_rope128(z, c128, s128) * qscale).astype(bf16)
    for h, z in enumerate(proj2(A_QW)):
        kr = _rope128(z, c128, s128)
        k_ref[:, h * LANES:(h + 1) * LANES] = kr
        kb_ref[:, h * LANES:(h + 1) * LANES] = kr.astype(bf16)
    for h, vv in enumerate(proj2(A_QW + A_KW)):
        v_ref[:, h * LANES:(h + 1) * LANES] = vv
        vb_ref[:, h * LANES:(h + 1) * LANES] = vv.astype(bf16)
    base = A_QW + 2 * A_KW
    for h in range(0, IDX_HEADS, 2):
        for hh, z in zip((h, h + 1), proj2(base + h * LANES)):
            qi_ref[hh] = _rope64(z, c64, s64).astype(bf16)
    base = base + IDX_HEADS * LANES
    zk, zw = proj2(base)
    kir = _rope64(zk, c64, s64)
    ki_ref[...] = kir[:, :IDX_DIM]
    kib_ref[...] = kir.astype(bf16)
    wi_ref[...] = zw * IDX_SCALE


def _a_project(x2d, w_pad, c128, s128, c64, s64, tb):
    R = x2d.shape[0]
    nblk = R // tb
    ntab = c128.shape[0] // tb
    tab = pl.BlockSpec((tb, LANES), lambda i: (i % ntab, 0))
    row = lambda w: pl.BlockSpec((tb, w), lambda i: (i, 0))
    hm = pl.BlockSpec((A_HEADS, tb, LANES), lambda i: (0, i, 0))
    return pl.pallas_call(
        _aproj_body,
        grid=(nblk,),
        in_specs=[row(D_MODEL), _const_spec((D_MODEL, A_PROJ_PAD)), tab, tab, tab, tab],
        out_specs=[hm, row(A_KW), row(A_KW), row(A_KW), row(A_KW), hm, row(IDX_DIM), row(LANES), row(LANES)],
        out_shape=[
            jax.ShapeDtypeStruct((A_HEADS, R, LANES), bf16),
            jax.ShapeDtypeStruct((R, A_KW), f32),
            jax.ShapeDtypeStruct((R, A_KW), f32),
            jax.ShapeDtypeStruct((R, A_KW), bf16),
            jax.ShapeDtypeStruct((R, A_KW), bf16),
            jax.ShapeDtypeStruct((IDX_HEADS, R, LANES), bf16),
            jax.ShapeDtypeStruct((R, IDX_DIM), f32),
            jax.ShapeDtypeStruct((R, LANES), bf16),
            jax.ShapeDtypeStruct((R, LANES), f32),
        ],
        compiler_params=_cparams(("arbitrary",)),
        name="dsa_project",
    )(x2d, w_pad, c128, s128, c64, s64)


def _mono_key(s):
    i = lax.bitcast_convert_type(s, i32)
    return i ^ ((i >> 31) & jnp.int32(INT_MAX))


def _dsa_body(q_ref, qi_ref, wi_ref, x_ref, k_hbm, v_hbm, ki_hbm, wo_ref, g_ref, b_ref,
              o_ref, k_vm, v_vm, ki_vm, keys_ref, m_ref, l_ref, alpha_ref, acc_ref, s_ref, p_ref, bias_ref, sem,
              *, tq, kb_size, nkb_total, q_pos0, l_real, topk):
    bi = pl.program_id(0)
    qi_blk = pl.program_id(1)

    @pl.when(qi_blk == 0)
    def _():
        copies = [pltpu.make_async_copy(k_hbm.at[bi], k_vm, sem.at[0]),
                  pltpu.make_async_copy(v_hbm.at[bi], v_vm, sem.at[1]),
                  pltpu.make_async_copy(ki_hbm.at[bi], ki_vm, sem.at[2])]
        for c in copies:
            c.start()
        for c in copies:
            c.wait()

    q0 = q_pos0 + qi_blk * tq
    last_chunk_end = ((q0 + tq - 1) // CHUNK + 1) * CHUNK
    nkb = jnp.minimum((last_chunk_end + kb_size - 1) // kb_size, nkb_total)

    qpos = q0 + lax.broadcasted_iota(i32, (tq, 1), 0)
    qchunk = qpos >> 6
    lane_k = lax.broadcasted_iota(i32, (1, kb_size), 1)

    def valid_mask(kb):
        kpos = kb * kb_size + lane_k
        ok = (kpos >> 6) <= qchunk
        return ok, kpos

    qi_all = qi_ref[...].reshape(IDX_HEADS * tq, LANES)
    wi = wi_ref[...]
    wcols = [wi[:, h:h + 1] for h in range(IDX_HEADS)]

    def score_block(kb, carry):
        off = pl.multiple_of(kb * kb_size, kb_size)
        kiblk = ki_vm[pl.ds(off, kb_size), :]
        rel = _dg(qi_all, kiblk, ((1,), (1,))).reshape(IDX_HEADS, tq, kb_size)
        score = jnp.maximum(rel[0], 0.0) * wcols[0]
        for h in range(1, IDX_HEADS):
            score = score + jnp.maximum(rel[h], 0.0) * wcols[h]
        adm, kpos = valid_mask(kb)
        key = _mono_key(jnp.where(adm, score, NEG_INF))
        if l_real < nkb_total * kb_size:
            key = jnp.where(kpos < l_real, key, jnp.int32(INT_MIN))
        keys_ref[kb] = key
        return carry

    lax.fori_loop(0, nkb, score_block, 0)

    nchunk = kb_size // LANES

    def count_where(pred):
        def body(kb, accv):
            for c in range(nchunk):
                blk = keys_ref[kb, :, c * LANES:(c + 1) * LANES]
                accv = accv + pred(blk, kb * kb_size + c * LANES).astype(i32)
            return accv
        accv = lax.fori_loop(0, nkb, body, jnp.zeros((tq, LANES), i32))
        return jnp.sum(accv, axis=1, keepdims=True)

    def bit_step(it, carry):
        p, cnt_p = carry
        bit = 31 - it
        cand = p | (jnp.int32(1) << bit)
        thr = jnp.broadcast_to(cand ^ jnp.int32(INT_MIN), (tq, LANES))
        cnt = count_where(lambda blk, _: blk >= thr)
        ok = cnt >= topk
        return jnp.where(ok, cand, p), jnp.where(ok, cnt, cnt_p)

    p_fin, cnt_ge = lax.fori_loop(0, 32, bit_step, (jnp.zeros((tq, 1), i32), jnp.zeros((tq, 1), i32)))
    thr = p_fin ^ jnp.int32(INT_MIN)
    thr_b = jnp.broadcast_to(thr, (tq, LANES))
    has_excess = cnt_ge > topk
    lane_i = lax.broadcasted_iota(i32, (tq, LANES), 1)

    def tie_limit():
        cnt_gt = count_where(lambda blk, _: blk > thr_b)
        need = topk - cnt_gt

        def idx_step(it, jj):
            bit = 14 - it
            cand = jj | (jnp.int32(1) << bit)
            cand_b = jnp.broadcast_to(cand, (tq, LANES))
            g = count_where(lambda blk, base: (blk == thr_b) & ((lane_i + base) < cand_b))
            return jnp.where(g < need, cand, jj)
        jj = lax.fori_loop(0, 15, idx_step, jnp.zeros((tq, 1), i32))
        return jnp.where(has_excess, jj, jnp.int32(INT_MAX))

    any_excess = jnp.max(has_excess.astype(f32)) > 0.0
    tie_j = lax.cond(any_excess, tie_limit, lambda: jnp.full((tq, 1), INT_MAX, i32))

    m_ref[...] = jnp.full(m_ref.shape, NEG_INF, f32)
    l_ref[...] = jnp.zeros(l_ref.shape, f32)
    acc_ref[...] = jnp.zeros(acc_ref.shape, f32)
    gq = A_GROUP * tq
    rc = min(tq, ATTN_ROW_CHUNK)

    def attn_block(kb, carry):
        off = pl.multiple_of(kb * kb_size, kb_size)
        key = keys_ref[kb]
        adm, kpos = valid_mask(kb)
        sel = (key > thr) | ((key == thr) & (kpos <= tie_j))
        mask = sel & adm
        if l_real < nkb_total * kb_size:
            mask = mask & (kpos < l_real)
        bias_ref[...] = jnp.where(mask, 0.0, NEG_INF)
        for g in range(A_KV_HEADS):
            qg = q_ref[g * A_GROUP:(g + 1) * A_GROUP].reshape(gq, LANES)
            kblk = k_vm[pl.ds(off, kb_size), g * LANES:(g + 1) * LANES]
            s_ref[g] = _dg(qg, kblk, ((1,), (1,)))
        for g in range(A_KV_HEADS):
            for r0 in range(0, gq, rc):
                qr0 = r0 % tq
                m_old = m_ref[g, r0:r0 + rc, :]
                parts = []
                mx = None
                for c in range(nchunk):
                    cs_ = slice(c * LANES, (c + 1) * LANES)
                    sc = s_ref[g, r0:r0 + rc, cs_] + bias_ref[qr0:qr0 + rc, cs_]
                    parts.append(sc)
                    mx = sc if mx is None else jnp.maximum(mx, sc)
                m_new = jnp.maximum(m_old, jnp.max(mx, axis=1, keepdims=True))
                alpha = jnp.exp2(m_old - m_new)
                psum = None
                for c in range(nchunk):
                    p = jnp.exp2(parts[c] - m_new)
                    p_ref[g, r0:r0 + rc, c * LANES:(c + 1) * LANES] = p.astype(bf16)
                    psum = p if psum is None else psum + p
                l_ref[g, r0:r0 + rc, :] = alpha * l_ref[g, r0:r0 + rc, :] + jnp.sum(psum, axis=1, keepdims=True)
                m_ref[g, r0:r0 + rc, :] = m_new
                alpha_ref[g, r0:r0 + rc, :] = alpha
            vblk = v_vm[pl.ds(off, kb_size), g * LANES:(g + 1) * LANES]
            acc_ref[g] = alpha_ref[g] * acc_ref[g] + jnp.dot(p_ref[g], vblk, preferred_element_type=f32)
        return carry

    lax.fori_loop(0, nkb, attn_block, 0)

    heads = []
    for g in range(A_KV_HEADS):
        og = acc_ref[g] / l_ref[g]
        for j in range(A_GROUP):
            heads.append(og[j * tq:(j + 1) * tq])
    o = jnp.concatenate(heads, axis=1)
    y = jnp.dot(o.astype(bf16), wo_ref[...], preferred_element_type=f32)
    o_ref[...] = _res_ln(x_ref[...], y, g_ref[...], b_ref[...])


def _dsa_attend(q_hm, qi_hm, wi, x2d, kb_all, vb_all, kib_all, w_out, g, b,
                *, n_batch, tq, kb_size, q_pos0, l_real, topk):
    R = x2d.shape[0]
    L = kb_all.shape[1]
    nq = R // (n_batch * tq)
    nkb_total = L // kb_size
    body = functools.partial(_dsa_body, tq=tq, kb_size=kb_size, nkb_total=nkb_total,
                             q_pos0=q_pos0, l_real=l_real, topk=topk)
    hm = pl.BlockSpec((A_HEADS, tq, LANES), lambda bi, qi: (0, bi * nq + qi, 0))
    row = lambda w: pl.BlockSpec((tq, w), lambda bi, qi: (bi * nq + qi, 0))
    anyspec = pl.BlockSpec(memory_space=pl.ANY)
    return pl.pallas_call(
        body,
        grid=(n_batch, nq),
        in_specs=[hm, hm, row(LANES), row(D_MODEL), anyspec, anyspec, anyspec,
                  _const_spec((A_QW, D_MODEL)), _const_spec((1, D_MODEL)), _const_spec((1, D_MODEL))],
        out_specs=row(D_MODEL),
        out_shape=jax.ShapeDtypeStruct((R, D_MODEL), f32),
        scratch_shapes=[
            pltpu.VMEM((L, A_KW), bf16),
            pltpu.VMEM((L, A_KW), bf16),
            pltpu.VMEM((L, LANES), bf16),
            pltpu.VMEM((nkb_total, tq, kb_size), i32),
            pltpu.VMEM((A_KV_HEADS, A_GROUP * tq, LANES), f32),
            pltpu.VMEM((A_KV_HEADS, A_GROUP * tq, LANES), f32),
            pltpu.VMEM((A_KV_HEADS, A_GROUP * tq, LANES), f32),
            pltpu.VMEM((A_KV_HEADS, A_GROUP * tq, LANES), f32),
            pltpu.VMEM((A_KV_HEADS, A_GROUP * tq, kb_size), f32),
            pltpu.VMEM((A_KV_HEADS, A_GROUP * tq, kb_size), bf16),
            pltpu.VMEM((tq, kb_size), f32),
            pltpu.SemaphoreType.DMA((3,)),
        ],
        compiler_params=_cparams(("arbitrary", "arbitrary")),
        name="dsa_attend",
    )(q_hm, qi_hm, wi, x2d, kb_all, vb_all, kib_all, w_out, g, b)


def _rglru_body(x_ref, h0_ref, cbuf_ref, win_ref, cw_ref, cb_ref, wa_ref, ba_ref, wx_ref, bx_ref,
                lam_ref, wout_ref, g_ref, b_ref,
                o_ref, hlast_ref, nbuf_ref,
                cconv_ref, hcar_ref, a_s, b_s, h_s, *, tb):
    t = pl.program_id(1)
    W = RNN_WIDTH

    @pl.when(t == 0)
    def _():
        cconv_ref[5:8, :] = cbuf_ref[0]
        hcar_ref[...] = h0_ref[0]

    x = x_ref[0]
    z = jnp.dot(x.astype(bf16), win_ref[...], preferred_element_type=f32)
    gate = z[:, :W]
    rec = z[:, W:]
    c3, c2, c1 = cconv_ref[5:6, :], cconv_ref[6:7, :], cconv_ref[7:8, :]
    r1, r2, r3 = _shift_rows(rec, [c3, c2, c1])
    u = (cw_ref[0:1, :] * r3 + cw_ref[1:2, :] * r2 + cw_ref[2:3, :] * r1 + cw_ref[3:4, :] * rec + cb_ref[...])
    cconv_ref[5:8, :] = rec[tb - 3:tb, :]
    nbuf_ref[0] = rec[tb - 3:tb, :]

    ub = u.astype(bf16)
    r = jax.nn.sigmoid(jnp.dot(ub, wa_ref[...], preferred_element_type=f32) + ba_ref[...])
    ig = jax.nn.sigmoid(jnp.dot(ub, wx_ref[...], preferred_element_type=f32) + bx_ref[...])
    log_a = (-LRU_C) * r * jax.nn.softplus(-lam_ref[...])
    a = jnp.exp(log_a)
    one_m_a2 = -jnp.tanh(log_a) * (a * a + 1.0)
    bv = jnp.sqrt(one_m_a2) * (ig * u)

    ng = tb // SUBLANES
    a3 = a.reshape(ng, SUBLANES, W)
    b3 = bv.reshape(ng, SUBLANES, W)
    sub = lax.broadcasted_iota(i32, (ng, SUBLANES, W), 1)
    off = 1
    while off < SUBLANES:
        m = sub >= off
        b3 = jnp.where(m, a3 * pltpu.roll(b3, off, axis=1) + b3, b3)
        a3 = jnp.where(m, a3 * pltpu.roll(a3, off, axis=1), a3)
        off *= 2
    a_s[...] = a3.reshape(tb, W)
    b_s[...] = b3.reshape(tb, W)

    def grp(gi, carry):
        r0 = pl.multiple_of(gi * SUBLANES, SUBLANES)
        hg = a_s[pl.ds(r0, SUBLANES), :] * carry + b_s[pl.ds(r0, SUBLANES), :]
        h_s[pl.ds(r0, SUBLANES), :] = hg
        return hg[SUBLANES - 1:SUBLANES, :]

    carry = lax.fori_loop(0, ng, grp, hcar_ref[...])
    hcar_ref[...] = carry
    hlast_ref[0] = carry

    y = jax.nn.gelu(gate) * h_s[...]
    out = jnp.dot(y.astype(bf16), wout_ref[...], preferred_element_type=f32)
    o_ref[0] = _res_ln(x, out, g_ref[...], b_ref[...])


def _mixer_b(x, h0, cbuf, w_in, conv_w, conv_b, wa, ba, wx, bx, lam, w_out, g, b, tb):
    B, T, _ = x.shape
    W = RNN_WIDTH
    nt = T // tb
    per_b = lambda r, c: pl.BlockSpec((1, r, c), lambda bi, ti: (bi, 0, 0))
    return pl.pallas_call(
        functools.partial(_rglru_body, tb=tb),
        grid=(B, nt),
        in_specs=[
            pl.BlockSpec((1, tb, D_MODEL), lambda bi, ti: (bi, ti, 0)),
            per_b(1, W), per_b(B_CONV - 1, W),
            _const_spec((D_MODEL, 2 * W)), _const_spec((B_CONV, W)), _const_spec((1, W)),
            _const_spec((W, W)), _const_spec((1, W)), _const_spec((W, W)), _const_spec((1, W)),
            _const_spec((1, W)), _const_spec((W, D_MODEL)), _const_spec((1, D_MODEL)), _const_spec((1, D_MODEL)),
        ],
        out_specs=[
            pl.BlockSpec((1, tb, D_MODEL), lambda bi, ti: (bi, ti, 0)),
            per_b(1, W), per_b(B_CONV - 1, W),
        ],
        out_shape=[jax.ShapeDtypeStruct((B, T, D_MODEL), f32),
                   jax.ShapeDtypeStruct((B, 1, W), f32),
                   jax.ShapeDtypeStruct((B, B_CONV - 1, W), f32)],
        scratch_shapes=[pltpu.VMEM((SUBLANES, W), f32), pltpu.VMEM((1, W), f32),
                        pltpu.VMEM((tb, W), f32), pltpu.VMEM((tb, W), f32), pltpu.VMEM((tb, W), f32)],
        compiler_params=_cparams(("arbitrary", "arbitrary")),
        name="rglru_mixer",
    )(x, h0, cbuf, w_in, conv_w, conv_b, wa, ba, wx, bx, lam, w_out, g, b)


C_HALO = 32


def _conformer_body(x_ref, cbuf_ref, win_ref, bin_ref, cw_ref, cb_ref, lg_ref, lb_ref, wout_ref, bout_ref,
                    g_ref, b_ref, o_ref, nbuf_ref, ext_ref, conv_ref, *, tb):
    t = pl.program_id(1)
    C = C_WIDTH
    hist = C_CONV - 1

    @pl.when(t == 0)
    def _():
        ext_ref[0:C_HALO - hist, :] = jnp.zeros((C_HALO - hist, C), f32)
        ext_ref[C_HALO - hist:C_HALO, :] = cbuf_ref[0]

    x = x_ref[0]
    z = jnp.dot(x.astype(bf16), win_ref[...], preferred_element_type=f32) + bin_ref[...]
    u = z[:, :C] * jax.nn.sigmoid(z[:, C:])
    ext_ref[C_HALO:C_HALO + tb, :] = u

    rt = min(tb, 128)
    for r0 in range(0, tb, rt):
        for c0 in range(0, C, LANES):
            acc = jnp.broadcast_to(cb_ref[:, c0:c0 + LANES], (rt, LANES))
            for k in range(C_CONV):
                acc = acc + cw_ref[k:k + 1, c0:c0 + LANES] * ext_ref[pl.ds(r0 + k + C_HALO - hist, rt), c0:c0 + LANES]
            conv_ref[r0:r0 + rt, c0:c0 + LANES] = acc

    nbuf_ref[0] = ext_ref[tb + C_HALO - hist:tb + C_HALO, :]
    ext_ref[0:C_HALO, :] = ext_ref[tb:tb + C_HALO, :]

    c = _layer_norm(conv_ref[...], lg_ref[...], lb_ref[...])
    s = jax.nn.silu(c)
    out = jnp.dot(s.astype(bf16), wout_ref[...], preferred_element_type=f32) + bout_ref[...]
    o_ref[0] = _res_ln(x, out, g_ref[...], b_ref[...])


def _mixer_c(x, cbuf, w_in, b_in, conv_w, conv_b, ln_g, ln_b, w_out, b_out, g, b, tb):
    B, T, _ = x.shape
    C = C_WIDTH
    nt = T // tb
    return pl.pallas_call(
        functools.partial(_conformer_body, tb=tb),
        grid=(B, nt),
        in_specs=[
            pl.BlockSpec((1, tb, D_MODEL), lambda bi, ti: (bi, ti, 0)),
            pl.BlockSpec((1, C_CONV - 1, C), lambda bi, ti: (bi, 0, 0)),
            _const_spec((D_MODEL, 2 * C)), _const_spec((1, 2 * C)),
            _const_spec((C_CONV, C)), _const_spec((1, C)), _const_spec((1, C)), _const_spec((1, C)),
            _const_spec((C, D_MODEL)), _const_spec((1, D_MODEL)),
            _const_spec((1, D_MODEL)), _const_spec((1, D_MODEL)),
        ],
        out_specs=[
            pl.BlockSpec((1, tb, D_MODEL), lambda bi, ti: (bi, ti, 0)),
            pl.BlockSpec((1, C_CONV - 1, C), lambda bi, ti: (bi, 0, 0)),
        ],
        out_shape=[jax.ShapeDtypeStruct((B, T, D_MODEL), f32),
                   jax.ShapeDtypeStruct((B, C_CONV - 1, C), f32)],
        scratch_shapes=[pltpu.VMEM((tb + C_HALO, C), f32), pltpu.VMEM((tb, C), f32)],
        compiler_params=_cparams(("arbitrary", "arbitrary")),
        name="conformer_mixer",
    )(x, cbuf, w_in, b_in, conv_w, conv_b, ln_g, ln_b, w_out, b_out, g, b)


def _rwkv_front_body(x_ref, sh_ref, mu_ref, wr_ref, wk_ref, wv_ref, w0_ref, w1_ref, w2_ref,
                     a0_ref, a1_ref, a2_ref, g1_ref, g2_ref, kk_ref, ka_ref, mseg_ref,
                     r_o, lw_o, k_o, v_o, an_o, bk_o, g_o, shout_ref, car_ref, *, tb):
    t = pl.program_id(1)

    @pl.when(t == 0)
    def _():
        car_ref[...] = sh_ref[0]

    x = x_ref[0]
    (prev,) = _shift_rows(x, [car_ref[...]])
    car_ref[...] = x[tb - 1:tb, :]
    shout_ref[0] = x[tb - 1:tb, :]
    xx = prev - x
    xr, xw, xk, xv, xa, xg = (x + xx * mu_ref[n:n + 1, :] for n in range(6))
    r = _mm(xr, wr_ref[...])
    k = _mm(xk, wk_ref[...])
    v = _mm(xv, wv_ref[...])
    wl = w0_ref[...] + _mm(jnp.tanh(_mm(xw, w1_ref[...])), w2_ref[...])
    w_log = -jax.nn.softplus(-wl) - 0.5
    lw = -jnp.exp(w_log)
    a = jax.nn.sigmoid(a0_ref[...] + _mm(_mm(xa, a1_ref[...]), a2_ref[...]))
    g = _mm(jax.nn.sigmoid(_mm(xg, g1_ref[...])), g2_ref[...])
    kk = k * kk_ref[...]
    n2 = _dot2(kk * kk, mseg_ref[...])
    kk = kk / jnp.maximum(jnp.sqrt(n2), 1e-12)
    k2 = k * (1.0 + (a - 1.0) * ka_ref[...])
    outs = ((r_o, r), (lw_o, lw), (k_o, k2), (v_o, v), (an_o, -kk), (bk_o, kk * a), (g_o, g))
    for ref, val in outs:
        for p in range(D_PAIRS):
            ref[0, p] = val[:, p * LANES:(p + 1) * LANES]


def _rwkv_front(x, shift0, mu, w_r, w_k, w_v, w0, w1, w2, a0, a1, a2, g1, g2, k_k, k_a, mseg, tb):
    B, T, D = x.shape
    nt = T // tb
    pm = pl.BlockSpec((1, D_PAIRS, tb, LANES), lambda bi, ti: (bi, 0, ti, 0))
    pm_shape = jax.ShapeDtypeStruct((B, D_PAIRS, T, LANES), f32)
    vec = _const_spec((1, D))
    return pl.pallas_call(
        functools.partial(_rwkv_front_body, tb=tb),
        grid=(B, nt),
        in_specs=[
            pl.BlockSpec((1, tb, D), lambda bi, ti: (bi, ti, 0)),
            pl.BlockSpec((1, 1, D), lambda bi, ti: (bi, 0, 0)),
            _const_spec((6, D)), _const_spec((D, D)), _const_spec((D, D)), _const_spec((D, D)),
            vec, _const_spec((D, LORA_PAD)), _const_spec((LORA_PAD, D)),
            vec, _const_spec((D, LORA_PAD)), _const_spec((LORA_PAD, D)),
            _const_spec((D, LORA_PAD)), _const_spec((LORA_PAD, D)),
            vec, vec, _const_spec((D, D)),
        ],
        out_specs=[pm] * 7 + [pl.BlockSpec((1, 1, D), lambda bi, ti: (bi, 0, 0))],
        out_shape=[pm_shape] * 7 + [jax.ShapeDtypeStruct((B, 1, D), f32)],
        scratch_shapes=[pltpu.VMEM((1, D), f32)],
        compiler_params=_cparams(("arbitrary", "arbitrary")),
        name="rwkv_front",
    )(x, shift0, mu, w_r, w_k, w_v, w0, w1, w2, a0, a1, a2, g1, g2, k_k, k_a, mseg)


def _wkv_body(r_ref, lw_ref, k_ref, v_ref, an_ref, bk_ref, g_ref, s0_ref, rk_ref, gng_ref, gnb_ref,
              y_ref, sout_ref, s_ref, *, tb):
    t = pl.program_id(1)
    C = WKV_CHUNK
    P2 = 2 * C

    @pl.when(t == 0)
    def _():
        s_ref[...] = s0_ref[0]

    lane = lax.broadcasted_iota(i32, (1, LANES), 1)
    m0 = (lane < D_HEAD_DIM).astype(f32)
    m1 = 1.0 - m0
    ri = lax.broadcasted_iota(i32, (P2, P2), 0)
    ci = lax.broadcasted_iota(i32, (P2, P2), 1)
    same_head = (ri >= C) == (ci >= C)
    s_idx = ri & (C - 1)
    t_idx = ci & (C - 1)
    mask_strict = same_head & (s_idx < t_idx)
    mask_incl = same_head & (s_idx <= t_idx)
    eye = (ri == ci).astype(f32)
    seg_ones = ((ri >= D_HEAD_DIM) == (ci >= D_HEAD_DIM)).astype(f32)
    tr = lax.broadcasted_iota(i32, (C, C), 0)
    tc = lax.broadcasted_iota(i32, (C, C), 1)
    tri_incl = (tc <= tr).astype(f32)

    def stack2(z):
        return jnp.concatenate([z * m0, z * m1], axis=0)

    dims_nt = ((1,), (1,))
    dims_tn = ((0,), (0,))
    inv = 1.0 / D_HEAD_DIM
    for c in range(tb // C):
        sl = slice(c * C, (c + 1) * C)
        PR = range(D_PAIRS)
        r = [r_ref[0, p, sl, :] for p in PR]
        lw = [lw_ref[0, p, sl, :] for p in PR]
        k = [k_ref[0, p, sl, :] for p in PR]
        v = [v_ref[0, p, sl, :] for p in PR]
        an = [an_ref[0, p, sl, :] for p in PR]
        bk = [bk_ref[0, p, sl, :] for p in PR]
        cs = [_dot2l(tri_incl, lw[p]) for p in PR]
        cs_end = [cs[p][C - 1:C, :] for p in PR]
        ah = [an[p] * jnp.exp(cs[p] - lw[p]) for p in PR]
        p_inv = [jnp.exp(-cs[p]) for p in PR]
        bh = [bk[p] * p_inv[p] for p in PR]
        kh = [k[p] * p_inv[p] for p in PR]
        rh = [r[p] * jnp.exp(cs[p]) for p in PR]
        p_tail = [jnp.exp(cs_end[p] - cs[p]) for p in PR]
        X = [jnp.concatenate([stack2(bh[p]), stack2(kh[p])], axis=0) for p in PR]
        Ga = [_dot3(X[p], jnp.concatenate([ah[p], ah[p]], axis=0), dims_nt) for p in PR]
        Gr = [_mm_nt(X[p], jnp.concatenate([rh[p], rh[p]], axis=0)) for p in PR]
        n_ab = [jnp.where(mask_strict, Ga[p][0:P2], 0.0) for p in PR]
        n_ak = [jnp.where(mask_strict, Ga[p][P2:2 * P2], 0.0) for p in PR]
        n_rb = [jnp.where(mask_incl, Gr[p][0:P2], 0.0).astype(bf16) for p in PR]
        n_rk = [jnp.where(mask_incl, Gr[p][P2:2 * P2], 0.0).astype(bf16) for p in PR]
        tt = [eye + n_ab[p] for p in PR]
        mpow = n_ab
        steps = 1
        while steps * 2 < C:
            mul = _dot3 if steps == 1 else _mm
            mpow = [mul(mpow[p], mpow[p]) for p in PR]
            tt = [tt[p] + mul(tt[p], mpow[p]) for p in PR]
            steps *= 2
        a2 = [stack2(ah[p]) for p in PR]
        r2 = [stack2(rh[p]) for p in PR]
        v2 = [stack2(v[p]) for p in PR]
        S = [s_ref[p] for p in PR]
        rhs = [_dot3(a2[p], S[p], dims_nt) + _dot3(n_ak[p], v2[p], dims_tn) for p in PR]
        u2 = [_dot3(tt[p], rhs[p], dims_tn) for p in PR]
        for p in PR:
            bt2 = stack2(bk[p] * p_tail[p])
            kt2 = stack2(k[p] * p_tail[p])
            s_ref[p] = S[p] * jnp.exp(cs_end[p]) + _dot3(u2[p], bt2, dims_tn) + _dot3(v2[p], kt2, dims_tn)
        y2 = [_mm_nt(r2[p], S[p]) + _dg(n_rb[p], u2[p].astype(bf16), dims_tn)
              + _dg(n_rk[p], v2[p].astype(bf16), dims_tn) for p in PR]
        for p in PR:
            y = y2[p][0:C] + y2[p][C:P2]
            mu_y = _dot2(y, seg_ones) * inv
            d = y - mu_y
            var = _dot2(d * d, seg_ones) * inv
            yn = d * lax.rsqrt(var + GN_EPS) * gng_ref[p] + gnb_ref[p]
            bonus = _dot2(r[p] * k[p] * rk_ref[p], seg_ones) * v[p]
            y_ref[0, p, sl, :] = (yn + bonus) * g_ref[0, p, sl, :]

    sout_ref[0] = s_ref[...]


def _wkv(r, lw, k, v, an, bk, g, s0, r_k, gn_g, gn_b, tb):
    B, NP, T, _ = r.shape
    nt = T // tb
    pm = pl.BlockSpec((1, NP, tb, LANES), lambda bi, ti: (bi, 0, ti, 0))
    st = pl.BlockSpec((1, NP, LANES, LANES), lambda bi, ti: (bi, 0, 0, 0))
    par = _const_spec((NP, 1, LANES))
    return pl.pallas_call(
        functools.partial(_wkv_body, tb=tb),
        grid=(B, nt),
        in_specs=[pm] * 7 + [st, par, par, par],
        out_specs=[pm, st],
        out_shape=[jax.ShapeDtypeStruct((B, NP, T, LANES), f32),
                   jax.ShapeDtypeStruct((B, NP, LANES, LANES), f32)],
        scratch_shapes=[pltpu.VMEM((NP, LANES, LANES), f32)],
        compiler_params=_cparams(("arbitrary", "arbitrary")),
        name="rwkv_wkv",
    )(r, lw, k, v, an, bk, g, s0, r_k, gn_g, gn_b)


def _rwkv_out_body(y_ref, x_ref, wo_ref, g_ref, b_ref, o_ref):
    y = jnp.concatenate([y_ref[0, p] for p in range(D_PAIRS)], axis=1)
    out = jnp.dot(y.astype(bf16), wo_ref[...], preferred_element_type=f32)
    o_ref[0] = _res_ln(x_ref[0], out, g_ref[...], b_ref[...])


def _rwkv_out(y_pm, x, w_o, g, b, tb):
    B, T, D = x.shape
    nt = T // tb
    return pl.pallas_call(
        _rwkv_out_body,
        grid=(B, nt),
        in_specs=[pl.BlockSpec((1, D_PAIRS, tb, LANES), lambda bi, ti: (bi, 0, ti, 0)),
                  pl.BlockSpec((1, tb, D), lambda bi, ti: (bi, ti, 0)),
                  _const_spec((D, D)), _const_spec((1, D)), _const_spec((1, D))],
        out_specs=pl.BlockSpec((1, tb, D), lambda bi, ti: (bi, ti, 0)),
        out_shape=jax.ShapeDtypeStruct((B, T, D), f32),
        compiler_params=_cparams(("arbitrary", "arbitrary")),
        name="rwkv_out",
    )(y_pm, x, w_o, g, b)


def _rope_tables(pos, head_dim, reps):
    half = head_dim // 2
    inv_freq = jnp.exp(-math.log(ROPE_THETA) * jnp.arange(half, dtype=f32) / half)
    ang = pos.astype(f32)[:, None] * inv_freq[None, :]
    cos, sin = jnp.cos(ang), jnp.sin(ang)
    cos_t = jnp.tile(jnp.concatenate([cos, cos], axis=1), (1, reps))
    sin_t = jnp.tile(jnp.concatenate([-sin, sin], axis=1), (1, reps))
    return cos_t, sin_t


def _pad_a_w_in(w):
    q = w[:, :A_QW]
    k = w[:, A_QW:A_QW + A_KW]
    v = w[:, A_QW + A_KW:A_QW + 2 * A_KW]
    o = A_QW + 2 * A_KW
    qi = w[:, o:o + IDX_HEADS * IDX_DIM].reshape(D_MODEL, IDX_HEADS, IDX_DIM)
    qi = jnp.pad(qi, ((0, 0), (0, 0), (0, LANES - IDX_DIM))).reshape(D_MODEL, IDX_HEADS * LANES)
    o += IDX_HEADS * IDX_DIM
    ki = jnp.pad(w[:, o:o + IDX_DIM], ((0, 0), (0, LANES - IDX_DIM)))
    o += IDX_DIM
    wi = jnp.pad(w[:, o:o + IDX_HEADS], ((0, 0), (0, LANES - IDX_HEADS)))
    return jnp.concatenate([q, k, v, qi, ki, wi], axis=1).astype(bf16)


def _block_diag(w):
    n, d, _ = w.shape
    eye = jnp.eye(n, dtype=w.dtype)
    return (eye[:, None, :, None] * w[:, :, None, :]).reshape(n * d, n * d)


def _pair_state(s):
    B = s.shape[0]
    s = s.reshape(B, D_PAIRS, 2, D_HEAD_DIM, D_HEAD_DIM)
    z = jnp.zeros_like(s[:, :, 0])
    top = jnp.concatenate([s[:, :, 0], z], axis=-1)
    bot = jnp.concatenate([z, s[:, :, 1]], axis=-1)
    return jnp.concatenate([top, bot], axis=-2)


def _unpair_state(sp):
    B = sp.shape[0]
    h0 = sp[:, :, :D_HEAD_DIM, :D_HEAD_DIM]
    h1 = sp[:, :, D_HEAD_DIM:, D_HEAD_DIM:]
    return jnp.stack([h0, h1], axis=2).reshape(B, D_HEADS, D_HEAD_DIM, D_HEAD_DIM)


def _row(v):
    return v.reshape(1, -1).astype(f32)


def _mixer_a_stream(x, w_pad, w_out, g, b, pos, cache, *, tb, tq, kb_size):
    B, T, D = x.shape
    R = B * T
    c128, s128 = _rope_tables(pos, A_HEAD_DIM, 1)
    c64, s64 = _rope_tables(pos, IDX_DIM, 2)
    if tb > T:
        rep = tb // T
        c128, s128, c64, s64 = (jnp.tile(a, (rep, 1)) for a in (c128, s128, c64, s64))
    x2d = x.reshape(R, D)
    q_hm, k_o, v_o, kb, vb, qi_hm, ki_o, kib, wi = _a_project(x2d, w_pad, c128, s128, c64, s64, tb)
    kb3, vb3, kib3 = kb.reshape(B, T, A_KW), vb.reshape(B, T, A_KW), kib.reshape(B, T, LANES)
    if cache is None:
        q_pos0, l_real = 0, T
    else:
        ck, cv, cki = cache
        P = ck.shape[1]
        kb3 = jnp.concatenate([ck.reshape(B, P, A_KW).astype(bf16), kb3], axis=1)
        vb3 = jnp.concatenate([cv.reshape(B, P, A_KW).astype(bf16), vb3], axis=1)
        cki_p = jnp.pad(cki, ((0, 0), (0, 0), (0, LANES - IDX_DIM))).astype(bf16)
        kib3 = jnp.concatenate([cki_p, kib3], axis=1)
        q_pos0, l_real = P, P + T
    l_pad = -(-l_real // kb_size) * kb_size
    if l_pad > l_real:
        padk = ((0, 0), (0, l_pad - l_real), (0, 0))
        kb3, vb3, kib3 = jnp.pad(kb3, padk), jnp.pad(vb3, padk), jnp.pad(kib3, padk)
    topk = min(TOPK_MAX, l_real // 4)
    xn = _dsa_attend(q_hm, qi_hm, wi, x2d, kb3, vb3, kib3, w_out, g, b,
                     n_batch=B, tq=tq, kb_size=kb_size, q_pos0=q_pos0, l_real=l_real, topk=topk)
    return (xn.reshape(B, T, D), k_o.reshape(B, T, A_KV_HEADS, A_HEAD_DIM),
            v_o.reshape(B, T, A_KV_HEADS, A_HEAD_DIM), ki_o.reshape(B, T, IDX_DIM))


def _mixer_d_stream(x, wkv0, shift0, prm, g, b, *, tb, tb_wkv):
    B, T, D = x.shape
    outs = _rwkv_front(x, shift0.reshape(B, 1, D), prm["mu"], prm["w_r"], prm["w_k"], prm["w_v"], prm["w0"],
                       prm["w1"], prm["w2"], prm["a0"], prm["a1"], prm["a2"], prm["g1"], prm["g2"],
                       prm["k_k"], prm["k_a"], prm["mseg"], tb)
    seq, shift_out = outs[:7], outs[7]
    t_pad = -(-T // tb_wkv) * tb_wkv
    if t_pad > T:
        seq = [jnp.pad(a, ((0, 0), (0, 0), (0, t_pad - T), (0, 0))) for a in seq]
    y_pm, s_out = _wkv(*seq, _pair_state(wkv0), prm["r_k"], prm["gn_g"], prm["gn_b"], tb_wkv)
    if t_pad > T:
        y_pm = y_pm[:, :, :T]
    xn = _rwkv_out(y_pm, x, prm["w_o"], g, b, tb)
    return xn, _unpair_state(s_out), shift_out.reshape(B, D)


def kernel(x_prompt, x_sample, cache_a_k, cache_a_v, cache_a_kidx, state_b_h, state_b_conv, state_c_conv,
           state_d_wkv, state_d_shift, state_f_conv, ln_g, ln_b, a_w_in, a_w_out,
           b_w_in, b_conv_w, b_conv_b, b_gate_a_w, b_gate_a_b, b_gate_x_w, b_gate_x_b, b_lambda, b_w_out,
           c_w_in, c_b_in, c_conv_w, c_conv_b, c_ln_g, c_ln_b, c_w_out, c_b_out,
           d_mu, d_w_r, d_w_k, d_w_v, d_w_o, d_w0, d_w1, d_w2, d_a0, d_a1, d_a2, d_g1, d_g2,
           d_k_k, d_k_a, d_r_k, d_gn_g, d_gn_b, f_w_up, f_conv_w, f_conv_b, f_w_down):
    xp, xs = x_prompt, x_sample
    BP, TP, D = xp.shape
    BS, TS, _ = xs.shape
    tb_p = 256
    tb_s = TS
    outs = {k: [] for k in ("ak_p", "ak_s", "av_p", "av_s", "aki_p", "aki_s", "bh_p", "bh_s", "bc_p", "bc_s",
                            "cc_p", "cc_s", "dw_p", "dw_s", "dsh_p", "dsh_s", "fc_p", "fc_s")}
    for i in range(DEPTH):
        m, j = i % 4, i // 4
        g0, b0 = _row(ln_g[i, 0]), _row(ln_b[i, 0])
        g1, b1 = _row(ln_g[i, 1]), _row(ln_b[i, 1])
        if m == 0:
            w_pad = _pad_a_w_in(a_w_in[j])
            w_out = a_w_out[j].astype(bf16)
            xp, kp, vp, kip = _mixer_a_stream(xp, w_pad, w_out, g0, b0, jnp.arange(TP, dtype=i32), None,
                                              tb=512, tq=128, kb_size=512)
            P = cache_a_k.shape[2]
            xs, ks, vs, kis = _mixer_a_stream(xs, w_pad, w_out, g0, b0, P + jnp.arange(TS, dtype=i32),
                                              (cache_a_k[j], cache_a_v[j], cache_a_kidx[j]),
                                              tb=BS * TS, tq=TS, kb_size=256)
            outs["ak_p"].append(kp); outs["ak_s"].append(ks)
            outs["av_p"].append(vp); outs["av_s"].append(vs)
            outs["aki_p"].append(kip); outs["aki_s"].append(kis)
        elif m == 1:
            W = RNN_WIDTH
            args = (b_w_in[j].astype(bf16), b_conv_w[j], _row(b_conv_b[j]),
                    _block_diag(b_gate_a_w[j]).astype(bf16), _row(b_gate_a_b[j]),
                    _block_diag(b_gate_x_w[j]).astype(bf16), _row(b_gate_x_b[j]),
                    _row(b_lambda[j]), b_w_out[j].astype(bf16), g0, b0)
            xp, hp, cp = _mixer_b(xp, jnp.zeros((BP, 1, W), f32), jnp.zeros((BP, B_CONV - 1, W), f32), *args, tb_p)
            xs, hs, cs = _mixer_b(xs, state_b_h[j].reshape(BS, 1, W), state_b_conv[j], *args, tb_s)
            outs["bh_p"].append(hp.reshape(BP, W)); outs["bh_s"].append(hs.reshape(BS, W))
            outs["bc_p"].append(cp); outs["bc_s"].append(cs)
        elif m == 2:
            args = (c_w_in[j].astype(bf16), _row(c_b_in[j]), c_conv_w[j], _row(c_conv_b[j]),
                    _row(c_ln_g[j]), _row(c_ln_b[j]), c_w_out[j].astype(bf16), _row(c_b_out[j]), g0, b0)
            xp, cp = _mixer_c(xp, jnp.zeros((BP, C_CONV - 1, C_WIDTH), f32), *args, tb_p)
            xs, cs = _mixer_c(xs, state_c_conv[j], *args, tb_s)
            outs["cc_p"].append(cp); outs["cc_s"].append(cs)
        else:
            lp = ((0, 0), (0, LORA_PAD - d_w1.shape[2]))
            lq = ((0, LORA_PAD - d_w1.shape[2]), (0, 0))
            gp = ((0, 0), (0, LORA_PAD - d_g1.shape[2]))
            gq = ((0, LORA_PAD - d_g1.shape[2]), (0, 0))
            pairs = lambda v: v.reshape(D_PAIRS, 1, LANES).astype(f32)
            prm = dict(
                mu=d_mu[j], w_r=d_w_r[j].astype(bf16), w_k=d_w_k[j].astype(bf16), w_v=d_w_v[j].astype(bf16),
                w_o=d_w_o[j].astype(bf16), w0=_row(d_w0[j]),
                w1=jnp.pad(d_w1[j], lp).astype(bf16), w2=jnp.pad(d_w2[j], lq).astype(bf16),
                a0=_row(d_a0[j]), a1=jnp.pad(d_a1[j], lp).astype(bf16), a2=jnp.pad(d_a2[j], lq).astype(bf16),
                g1=jnp.pad(d_g1[j], gp).astype(bf16), g2=jnp.pad(d_g2[j], gq).astype(bf16),
                k_k=_row(d_k_k[j]), k_a=_row(d_k_a[j]),
                mseg=_block_diag(jnp.ones((D_HEADS, D_HEAD_DIM, D_HEAD_DIM), bf16)),
                r_k=pairs(d_r_k[j]), gn_g=pairs(d_gn_g[j]), gn_b=pairs(d_gn_b[j]))
            xp, sp, shp = _mixer_d_stream(xp, jnp.zeros((BP, D_HEADS, D_HEAD_DIM, D_HEAD_DIM), f32),
                                          jnp.zeros((BP, D), f32), prm, g0, b0, tb=tb_p, tb_wkv=WKV_CHUNK)
            xs, ss, shs = _mixer_d_stream(xs, state_d_wkv[j], state_d_shift[j], prm, g0, b0,
                                          tb=tb_s, tb_wkv=WKV_CHUNK)
            outs["dw_p"].append(sp); outs["dw_s"].append(ss)
            outs["dsh_p"].append(shp); outs["dsh_s"].append(shs)
        fargs = (f_w_up[i].astype(bf16), f_conv_w[i], _row(f_conv_b[i]), f_w_down[i].astype(bf16), g1, b1)
        xp, fbp = _conv_ffn(xp, jnp.zeros((BP, F_CONV - 1, 2 * D_FF), f32), *fargs, tb_p)
        xs, fbs = _conv_ffn(xs, state_f_conv[i], *fargs, tb_s)
        outs["fc_p"].append(fbp); outs["fc_s"].append(fbs)
    st = lambda k: jnp.stack(outs[k])
    return (xp, xs,
            st("ak_p"), st("ak_s"), st("av_p"), st("av_s"), st("aki_p"), st("aki_s"),
            st("bh_p"), st("bh_s"), st("bc_p"), st("bc_s"), st("cc_p"), st("cc_s"),
            st("dw_p"), st("dw_s"), st("dsh_p"), st("dsh_s"), st("fc_p"), st("fc_s"))
```

```python
import functools
import math

import jax
import jax.numpy as jnp
from jax import lax
from jax.experimental import pallas as pl
from jax.experimental.pallas import tpu as pltpu

f32 = jnp.float32
bf16 = jnp.bfloat16
i32 = jnp.int32

D_MODEL = 1024
DEPTH = 4
CHUNK = 64
ALPHA = (2.0 * DEPTH) ** 0.25
LN_EPS = 1e-5
NEG_INF = -1e30

A_HEADS = 8
A_KV_HEADS = 2
A_HEAD_DIM = 128
A_GROUP = A_HEADS // A_KV_HEADS
IDX_HEADS = 8
IDX_DIM = 64
IDX_SCALE = (IDX_DIM ** -0.5) * (IDX_HEADS ** -0.5)
TOPK_MAX = 256
ROPE_THETA = 10000.0

RNN_WIDTH = 1408
LRU_BLOCKS = 8
LRU_BLOCK_DIM = RNN_WIDTH // LRU_BLOCKS
B_CONV = 4
LRU_C = 8.0

C_WIDTH = 1024
C_CONV = 31

D_HEAD_DIM = 64
D_HEADS = D_MODEL // D_HEAD_DIM
D_PAIRS = D_HEADS // 2
GN_EPS = 64e-5
LORA_PAD = 128

D_FF = 2816
F_CONV = 3

LANES = 128
SUBLANES = 8
VMEM_LIMIT_BYTES = 60 * 1024 * 1024
WKV_CHUNK = 64

INT_MIN = -(2 ** 31)
INT_MAX = 2 ** 31 - 1


def _cparams(sem):
    return pltpu.CompilerParams(dimension_semantics=sem, vmem_limit_bytes=VMEM_LIMIT_BYTES)


def _const_spec(shape):
    nd = len(shape)
    return pl.BlockSpec(shape, lambda *_: (0,) * nd, pipeline_mode=pl.Buffered(1))


def _mm(a, b):
    return jnp.dot(a.astype(bf16), b.astype(bf16), preferred_element_type=f32)


def _split(a):
    hi = a.astype(bf16)
    lo = (a - hi.astype(f32)).astype(bf16)
    return hi, lo


def _dg(a, b, dims):
    return lax.dot_general(a, b, (dims, ((), ())), preferred_element_type=f32)


def _mm_nt(a, b):
    return _dg(a.astype(bf16), b.astype(bf16), ((1,), (1,)))


def _dot3(a, b, dims=((1,), (0,))):
    ah, al = _split(a)
    bh, bl = _split(b)
    return _dg(ah, bh, dims) + (_dg(ah, bl, dims) + _dg(al, bh, dims))


def _dot2(a, b_exact, dims=((1,), (0,))):
    ah, al = _split(a)
    bb = b_exact.astype(bf16)
    return _dg(ah, bb, dims) + _dg(al, bb, dims)


def _dot2l(a_exact, b, dims=((1,), (0,))):
    bh, bl = _split(b)
    aa = a_exact.astype(bf16)
    return _dg(aa, bh, dims) + _dg(aa, bl, dims)


def _layer_norm(z, g, b, eps=LN_EPS):
    mu = jnp.mean(z, axis=-1, keepdims=True)
    zc = z - mu
    var = jnp.mean(zc * zc, axis=-1, keepdims=True)
    return zc * lax.rsqrt(var + eps) * g + b


def _res_ln(x, y, g, b):
    return _layer_norm(ALPHA * x + y, g, b)


def _shift_rows(h, carry_rows):
    n = len(carry_rows)
    row = lax.broadcasted_iota(i32, (SUBLANES, h.shape[1]), 0)
    out = []
    for k in range(1, n + 1):
        hk = pltpu.roll(h, k, axis=0)
        top = hk[0:SUBLANES]
        for j in range(k):
            top = jnp.where(row == j, carry_rows[n - (k - j)], top)
        out.append(jnp.concatenate([top, hk[SUBLANES:]], axis=0) if h.shape[0] > SUBLANES else top)
    return out


FFN_TN = 256


def _ffn_body(x_ref, buf_ref, wup_ref, cw_ref, cb_ref, wdn_ref, g_ref, b_ref,
              o_ref, nbuf_ref, carry_ref, *, tb):
    t = pl.program_id(1)

    @pl.when(t == 0)
    def _():
        carry_ref[6:8, :] = buf_ref[0]

    x = x_ref[0]
    xb = x.astype(bf16)
    nj = D_FF // FFN_TN

    def up(j):
        return [jnp.dot(xb, wup_ref[:, part * D_FF + j * FFN_TN: part * D_FF + (j + 1) * FFN_TN],
                        preferred_element_type=f32) for part in range(2)]

    acc = jnp.zeros((tb, D_MODEL), f32)
    hs_next = up(0)
    for j in range(nj):
        hs = hs_next
        if j + 1 < nj:
            hs_next = up(j + 1)
        us = []
        for part in range(2):
            c0 = part * D_FF + j * FFN_TN
            h = hs[part]
            cm2 = carry_ref[6:7, c0:c0 + FFN_TN]
            cm1 = carry_ref[7:8, c0:c0 + FFN_TN]
            h1, h2 = _shift_rows(h, [cm2, cm1])
            u = (cw_ref[0:1, c0:c0 + FFN_TN] * h2 + cw_ref[1:2, c0:c0 + FFN_TN] * h1
                 + cw_ref[2:3, c0:c0 + FFN_TN] * h + cb_ref[:, c0:c0 + FFN_TN])
            carry_ref[6:8, c0:c0 + FFN_TN] = h[tb - 2:tb, :]
            us.append(u)
        act = jax.nn.silu(us[0]) * us[1]
        acc = acc + jnp.dot(act.astype(bf16), wdn_ref[j * FFN_TN:(j + 1) * FFN_TN, :],
                            preferred_element_type=f32)
    nbuf_ref[0] = carry_ref[6:8, :]
    o_ref[0] = _res_ln(x, acc, g_ref[...], b_ref[...])


def _conv_ffn(x, buf, w_up, conv_w, conv_b, w_down, g, b, tb):
    B, T, _ = x.shape
    nt = T // tb
    return pl.pallas_call(
        functools.partial(_ffn_body, tb=tb),
        grid=(B, nt),
        in_specs=[
            pl.BlockSpec((1, tb, D_MODEL), lambda bi, ti: (bi, ti, 0)),
            pl.BlockSpec((1, F_CONV - 1, 2 * D_FF), lambda bi, ti: (bi, 0, 0)),
            _const_spec((D_MODEL, 2 * D_FF)),
            _const_spec((F_CONV, 2 * D_FF)),
            _const_spec((1, 2 * D_FF)),
            _const_spec((D_FF, D_MODEL)),
            _const_spec((1, D_MODEL)),
            _const_spec((1, D_MODEL)),
        ],
        out_specs=[
            pl.BlockSpec((1, tb, D_MODEL), lambda bi, ti: (bi, ti, 0)),
            pl.BlockSpec((1, F_CONV - 1, 2 * D_FF), lambda bi, ti: (bi, 0, 0)),
        ],
        out_shape=[jax.ShapeDtypeStruct((B, T, D_MODEL), f32),
                   jax.ShapeDtypeStruct((B, F_CONV - 1, 2 * D_FF), f32)],
        scratch_shapes=[pltpu.VMEM((SUBLANES, 2 * D_FF), f32)],
        compiler_params=_cparams(("arbitrary", "arbitrary")),
        name="conv_ffn",
    )(x, buf, w_up, conv_w, conv_b, w_down, g, b)


A_QW = A_HEADS * A_HEAD_DIM
A_KW = A_KV_HEADS * A_HEAD_DIM
A_PROJ_PAD = A_QW + 2 * A_KW + IDX_HEADS * LANES + LANES + LANES
LOG2_E = math.log2(math.e)
TOPM = 12
TOPM_ROWS = 16
ATTN_ROW_CHUNK = 32


def _rope128(z, cos, sin):
    return z * cos + pltpu.roll(z, A_HEAD_DIM // 2, axis=1) * sin


def _rope64(z, cos, sin):
    lane = lax.broadcasted_iota(i32, z.shape, 1)
    rot = jnp.where((lane & 32) == 0, pltpu.roll(z, LANES - 32, axis=1), pltpu.roll(z, 32, axis=1))
    return z * cos + rot * sin


def _aproj_body(x_ref, w_ref, c128_ref, s128_ref, c64_ref, s64_ref,
                q_ref, k_ref, v_ref, kb_ref, vb_ref, qi_ref, ki_ref, kib_ref, wi_ref):
    xb = x_ref[...].astype(bf16)
    c128, s128 = c128_ref[...], s128_ref[...]
    c64, s64 = c64_ref[...], s64_ref[...]
    qscale = A_HEAD_DIM ** -0.5 * LOG2_E

    def proj2(c0):
        z = jnp.dot(xb, w_ref[:, c0:c0 + 2 * LANES], preferred_element_type=f32)
        return z[:, :LANES], z[:, LANES:]

    for h in range(0, A_HEADS, 2):
        for hh, z in zip((h, h + 1), proj2(h * LANES)):
            q_ref[hh] = (_rope128(z, c128, s128) * qscale).astype(bf16)
    for h, z in enumerate(proj2(A_QW)):
        kr = _rope128(z, c128, s128)
        k_ref[:, h * LANES:(h + 1) * LANES] = kr
        kb_ref[:, h * LANES:(h + 1) * LANES] = kr.astype(bf16)
    for h, vv in enumerate(proj2(A_QW + A_KW)):
        v_ref[:, h * LANES:(h + 1) * LANES] = vv
        vb_ref[:, h * LANES:(h + 1) * LANES] = vv.astype(bf16)
    base = A_QW + 2 * A_KW
    for h in range(0, IDX_HEADS, 2):
        for hh, z in zip((h, h + 1), proj2(base + h * LANES)):
            qi_ref[hh] = _rope64(z, c64, s64).astype(bf16)
    base = base + IDX_HEADS * LANES
    zk, zw = proj2(base)
    kir = _rope64(zk, c64, s64)
    ki_ref[...] = kir[:, :IDX_DIM]
    kib_ref[...] = kir.astype(bf16)
    wi_ref[...] = zw * IDX_SCALE


def _a_project(x2d, w_pad, c128, s128, c64, s64, tb):
    R = x2d.shape[0]
    nblk = R // tb
    ntab = c128.shape[0] // tb
    tab = pl.BlockSpec((tb, LANES), lambda i: (i % ntab, 0))
    row = lambda w: pl.BlockSpec((tb, w), lambda i: (i, 0))
    hm = pl.BlockSpec((A_HEADS, tb, LANES), lambda i: (0, i, 0))
    return pl.pallas_call(
        _aproj_body,
        grid=(nblk,),
        in_specs=[row(D_MODEL), _const_spec((D_MODEL, A_PROJ_PAD)), tab, tab, tab, tab],
        out_specs=[hm, row(A_KW), row(A_KW), row(A_KW), row(A_KW), hm, row(IDX_DIM), row(LANES), row(LANES)],
        out_shape=[
            jax.ShapeDtypeStruct((A_HEADS, R, LANES), bf16),
            jax.ShapeDtypeStruct((R, A_KW), f32),
            jax.ShapeDtypeStruct((R, A_KW), f32),
            jax.ShapeDtypeStruct((R, A_KW), bf16),
            jax.ShapeDtypeStruct((R, A_KW), bf16),
            jax.ShapeDtypeStruct((IDX_HEADS, R, LANES), bf16),
            jax.ShapeDtypeStruct((R, IDX_DIM), f32),
            jax.ShapeDtypeStruct((R, LANES), bf16),
            jax.ShapeDtypeStruct((R, LANES), f32),
        ],
        compiler_params=_cparams(("arbitrary",)),
        name="dsa_project",
    )(x2d, w_pad, c128, s128, c64, s64)


def _mono_key(s):
    i = lax.bitcast_convert_type(s, i32)
    return i ^ ((i >> 31) & jnp.int32(INT_MAX))


def _key_to_f32(k):
    return lax.bitcast_convert_type(k ^ ((k >> 31) & jnp.int32(INT_MAX)), f32)


KEY_PAD = -(2 ** 31) + 0x7FFFFF


def _dsa_body(q_ref, qi_ref, wi_ref, x_ref, k_hbm, v_hbm, ki_hbm, wo_ref, g_ref, b_ref,
              o_ref, k_vm, v_vm, ki_vm, keys_ref, cand_ref, m_ref, l_ref, alpha_ref, acc_ref, s_ref, p_ref, bias_ref,
              sem,
              *, tq, kb_size, nkb_total, q_pos0, l_real, topk):
    bi = pl.program_id(0)
    qi_blk = pl.program_id(1)

    @pl.when(qi_blk == 0)
    def _():
        copies = [pltpu.make_async_copy(k_hbm.at[bi], k_vm, sem.at[0]),
                  pltpu.make_async_copy(v_hbm.at[bi], v_vm, sem.at[1]),
                  pltpu.make_async_copy(ki_hbm.at[bi], ki_vm, sem.at[2])]
        for c in copies:
            c.start()
        for c in copies:
            c.wait()

    q0 = q_pos0 + qi_blk * tq
    last_chunk_end = ((q0 + tq - 1) // CHUNK + 1) * CHUNK
    nkb = jnp.minimum((last_chunk_end + kb_size - 1) // kb_size, nkb_total)

    qpos = q0 + lax.broadcasted_iota(i32, (tq, 1), 0)
    qchunk = qpos >> 6
    lane_k = lax.broadcasted_iota(i32, (1, kb_size), 1)

    def valid_mask(kb):
        kpos = kb * kb_size + lane_k
        ok = (kpos >> 6) <= qchunk
        return ok, kpos

    qi_all = qi_ref[...].reshape(IDX_HEADS * tq, LANES)
    wi = wi_ref[...]
    wcols = [wi[:, h:h + 1] for h in range(IDX_HEADS)]

    def score_block(kb):
        off = pl.multiple_of(kb * kb_size, kb_size)
        kiblk = ki_vm[pl.ds(off, kb_size), :]
        rel = _dg(qi_all, kiblk, ((1,), (1,))).reshape(IDX_HEADS, tq, kb_size)
        score = jnp.maximum(rel[0], 0.0) * wcols[0]
        for h in range(1, IDX_HEADS):
            score = score + jnp.maximum(rel[h], 0.0) * wcols[h]
        adm, kpos = valid_mask(kb)
        score = jnp.where(score == 0.0, 0.0, score)
        key = _mono_key(jnp.where(adm, score, NEG_INF))
        if l_real < nkb_total * kb_size:
            key = jnp.where(kpos < l_real, key, jnp.int32(KEY_PAD))
        keys_ref[kb] = key

    def score_pair(i, carry):
        score_block(2 * i)
        score_block(jnp.minimum(2 * i + 1, nkb - 1))
        return carry

    lax.fori_loop(0, (nkb + 1) // 2, score_pair, 0)

    nchunk = kb_size // LANES

    def count_where(pred):
        def body(kb, accv):
            for c in range(nchunk):
                blk = keys_ref[kb, :, c * LANES:(c + 1) * LANES]
                accv = accv + pred(blk, kb * kb_size + c * LANES).astype(i32)
            return accv
        accv = lax.fori_loop(0, nkb, body, jnp.zeros((tq, LANES), i32))
        return jnp.sum(accv, axis=1, keepdims=True)

    def bisect(count_ge):
        def bit_step(it, carry):
            p, cnt_p = carry
            bit = 31 - it
            cand = p | (jnp.int32(1) << bit)
            cnt = count_ge(jnp.broadcast_to(cand ^ jnp.int32(INT_MIN), (tq, LANES)))
            ok = cnt >= topk
            return jnp.where(ok, cand, p), jnp.where(ok, cnt, cnt_p)
        return lax.fori_loop(0, 32, bit_step, (jnp.zeros((tq, 1), i32), jnp.zeros((tq, 1), i32)))

    def lane_topm(rg, carry):
        r0 = pl.multiple_of(rg * TOPM_ROWS, TOPM_ROWS)

        def body(kb, lists):
            for c in range(nchunk):
                x = _key_to_f32(keys_ref[kb, pl.ds(r0, TOPM_ROWS), c * LANES:(c + 1) * LANES])
                new = []
                for li in lists:
                    new.append(jnp.maximum(li, x))
                    x = jnp.minimum(li, x)
                lists = tuple(new)
            return lists

        init = tuple(jnp.full((TOPM_ROWS, LANES), -jnp.inf, f32) for _ in range(TOPM))
        lists = lax.fori_loop(0, nkb, body, init)
        for i in range(TOPM):
            cand_ref[i, pl.ds(r0, TOPM_ROWS), :] = _mono_key(lists[i])
        return carry

    lax.fori_loop(0, tq // TOPM_ROWS, lane_topm, 0)

    def count_cand_ge(thr_b):
        accv = (cand_ref[0] >= thr_b).astype(i32)
        for i in range(1, TOPM):
            accv = accv + (cand_ref[i] >= thr_b).astype(i32)
        return jnp.sum(accv, axis=1, keepdims=True)

    p_c, cnt_c = bisect(count_cand_ge)
    thr_c = jnp.broadcast_to(p_c ^ jnp.int32(INT_MIN), (tq, LANES))
    min_kept = cand_ref[TOPM - 1]
    covered = (min_kept < thr_c) | (min_kept == jnp.int32(KEY_PAD))
    all_covered = jnp.min(covered.astype(f32)) > 0.0
    p_fin, cnt_ge = lax.cond(all_covered, lambda: (p_c, cnt_c),
                             lambda: bisect(lambda thr_b: count_where(lambda blk, _: blk >= thr_b)))
    thr = p_fin ^ jnp.int32(INT_MIN)
    thr_b = jnp.broadcast_to(thr, (tq, LANES))
    has_excess = cnt_ge > topk
    lane_i = lax.broadcasted_iota(i32, (tq, LANES), 1)

    def tie_limit():
        cnt_gt = count_where(lambda blk, _: blk > thr_b)
        need = topk - cnt_gt

        def idx_step(it, jj):
            bit = 14 - it
            cand = jj | (jnp.int32(1) << bit)
            cand_b = jnp.broadcast_to(cand, (tq, LANES))
            g = count_where(lambda blk, base: (blk == thr_b) & ((lane_i + base) < cand_b))
            return jnp.where(g < need, cand, jj)
        jj = lax.fori_loop(0, 15, idx_step, jnp.zeros((tq, 1), i32))
        return jnp.where(has_excess, jj, jnp.int32(INT_MAX))

    any_excess = jnp.max(has_excess.astype(f32)) > 0.0
    tie_j = lax.cond(any_excess, tie_limit, lambda: jnp.full((tq, 1), INT_MAX, i32))

    m_ref[...] = jnp.full(m_ref.shape, NEG_INF, f32)
    l_ref[...] = jnp.zeros(l_ref.shape, f32)
    acc_ref[...] = jnp.zeros(acc_ref.shape, f32)
    gq = A_GROUP * tq
    rc = min(tq, ATTN_ROW_CHUNK)

    def attn_block(kb, carry):
        off = pl.multiple_of(kb * kb_size, kb_size)
        key = keys_ref[kb]
        adm, kpos = valid_mask(kb)
        sel = (key > thr) | ((key == thr) & (kpos <= tie_j))
        mask = sel & adm
        if l_real < nkb_total * kb_size:
            mask = mask & (kpos < l_real)
        bias_ref[...] = jnp.where(mask, 0.0, NEG_INF)
        for g in range(A_KV_HEADS):
            qg = q_ref[g * A_GROUP:(g + 1) * A_GROUP].reshape(gq, LANES)
            kblk = k_vm[pl.ds(off, kb_size), g * LANES:(g + 1) * LANES]
            s_ref[g] = _dg(qg, kblk, ((1,), (1,)))
        for g in range(A_KV_HEADS):
            for r0 in range(0, gq, rc):
                qr0 = r0 % tq
                m_old = m_ref[g, r0:r0 + rc, :]
                parts = []
                mx = None
                for c in range(nchunk):
                    cs_ = slice(c * LANES, (c + 1) * LANES)
                    sc = s_ref[g, r0:r0 + rc, cs_] + bias_ref[qr0:qr0 + rc, cs_]
                    parts.append(sc)
                    mx = sc if mx is None else jnp.maximum(mx, sc)
                m_new = jnp.maximum(m_old, jnp.max(mx, axis=1, keepdims=True))
                alpha = jnp.exp2(m_old - m_new)
                psum = None
                for c in range(nchunk):
                    p = jnp.exp2(parts[c] - m_new)
                    p_ref[g, r0:r0 + rc, c * LANES:(c + 1) * LANES] = p.astype(bf16)
                    psum = p if psum is None else psum + p
                l_ref[g, r0:r0 + rc, :] = alpha * l_ref[g, r0:r0 + rc, :] + jnp.sum(psum, axis=1, keepdims=True)
                m_ref[g, r0:r0 + rc, :] = m_new
                alpha_ref[g, r0:r0 + rc, :] = alpha
            vblk = v_vm[pl.ds(off, kb_size), g * LANES:(g + 1) * LANES]
            acc_ref[g] = alpha_ref[g] * acc_ref[g] + jnp.dot(p_ref[g], vblk, preferred_element_type=f32)
        return carry

    lax.fori_loop(0, nkb, attn_block, 0)

    heads = []
    for g in range(A_KV_HEADS):
        og = acc_ref[g] / l_ref[g]
        for j in range(A_GROUP):
            heads.append(og[j * tq:(j + 1) * tq])
    o = jnp.concatenate(heads, axis=1)
    y = jnp.dot(o.astype(bf16), wo_ref[...], preferred_element_type=f32)
    o_ref[...] = _res_ln(x_ref[...], y, g_ref[...], b_ref[...])


def _dsa_attend(q_hm, qi_hm, wi, x2d, kb_all, vb_all, kib_all, w_out, g, b,
                *, n_batch, tq, kb_size, q_pos0, l_real, topk):
    R = x2d.shape[0]
    L = kb_all.shape[1]
    nq = R // (n_batch * tq)
    nkb_total = L // kb_size
    body = functools.partial(_dsa_body, tq=tq, kb_size=kb_size, nkb_total=nkb_total,
                             q_pos0=q_pos0, l_real=l_real, topk=topk)
    hm = pl.BlockSpec((A_HEADS, tq, LANES), lambda bi, qi: (0, bi * nq + qi, 0))
    row = lambda w: pl.BlockSpec((tq, w), lambda bi, qi: (bi * nq + qi, 0))
    anyspec = pl.BlockSpec(memory_space=pl.ANY)
    return pl.pallas_call(
        body,
        grid=(n_batch, nq),
        in_specs=[hm, hm, row(LANES), row(D_MODEL), anyspec, anyspec, anyspec,
                  _const_spec((A_QW, D_MODEL)), _const_spec((1, D_MODEL)), _const_spec((1, D_MODEL))],
        out_specs=row(D_MODEL),
        out_shape=jax.ShapeDtypeStruct((R, D_MODEL), f32),
        scratch_shapes=[
            pltpu.VMEM((L, A_KW), bf16),
            pltpu.VMEM((L, A_KW), bf16),
            pltpu.VMEM((L, LANES), bf16),
            pltpu.VMEM((nkb_total, tq, kb_size), i32),
            pltpu.VMEM((TOPM, tq, LANES), i32),
            pltpu.VMEM((A_KV_HEADS, A_GROUP * tq, LANES), f32),
            pltpu.VMEM((A_KV_HEADS, A_GROUP * tq, LANES), f32),
            pltpu.VMEM((A_KV_HEADS, A_GROUP * tq, LANES), f32),
            pltpu.VMEM((A_KV_HEADS, A_GROUP * tq, LANES), f32),
            pltpu.VMEM((A_KV_HEADS, A_GROUP * tq, kb_size), f32),
            pltpu.VMEM((A_KV_HEADS, A_GROUP * tq, kb_size), bf16),
            pltpu.VMEM((tq, kb_size), f32),
            pltpu.SemaphoreType.DMA((3,)),
        ],
        compiler_params=_cparams(("arbitrary", "arbitrary")),
        name="dsa_attend",
    )(q_hm, qi_hm, wi, x2d, kb_all, vb_all, kib_all, w_out, g, b)


def _rglru_body(x_ref, h0_ref, cbuf_ref, win_ref, cw_ref, cb_ref, wa_ref, ba_ref, wx_ref, bx_ref,
                lam_ref, wout_ref, g_ref, b_ref,
                o_ref, hlast_ref, nbuf_ref,
                cconv_ref, hcar_ref, a_s, b_s, h_s, *, tb):
    t = pl.program_id(1)
    W = RNN_WIDTH

    @pl.when(t == 0)
    def _():
        cconv_ref[5:8, :] = cbuf_ref[0]
        hcar_ref[...] = h0_ref[0]

    x = x_ref[0]
    z = jnp.dot(x.astype(bf16), win_ref[...], preferred_element_type=f32)
    gate = z[:, :W]
    rec = z[:, W:]
    c3, c2, c1 = cconv_ref[5:6, :], cconv_ref[6:7, :], cconv_ref[7:8, :]
    r1, r2, r3 = _shift_rows(rec, [c3, c2, c1])
    u = (cw_ref[0:1, :] * r3 + cw_ref[1:2, :] * r2 + cw_ref[2:3, :] * r1 + cw_ref[3:4, :] * rec + cb_ref[...])
    cconv_ref[5:8, :] = rec[tb - 3:tb, :]
    nbuf_ref[0] = rec[tb - 3:tb, :]

    ub = u.astype(bf16)
    r = jax.nn.sigmoid(jnp.dot(ub, wa_ref[...], preferred_element_type=f32) + ba_ref[...])
    ig = jax.nn.sigmoid(jnp.dot(ub, wx_ref[...], preferred_element_type=f32) + bx_ref[...])
    log_a = (-LRU_C) * r * jax.nn.softplus(-lam_ref[...])
    a = jnp.exp(log_a)
    one_m_a2 = -jnp.tanh(log_a) * (a * a + 1.0)
    bv = jnp.sqrt(one_m_a2) * (ig * u)

    ng = tb // SUBLANES
    a3 = a.reshape(ng, SUBLANES, W)
    b3 = bv.reshape(ng, SUBLANES, W)
    sub = lax.broadcasted_iota(i32, (ng, SUBLANES, W), 1)
    off = 1
    while off < SUBLANES:
        m = sub >= off
        b3 = jnp.where(m, a3 * pltpu.roll(b3, off, axis=1) + b3, b3)
        a3 = jnp.where(m, a3 * pltpu.roll(a3, off, axis=1), a3)
        off *= 2
    a_s[...] = a3.reshape(tb, W)
    b_s[...] = b3.reshape(tb, W)

    def grp(gi, carry):
        r0 = pl.multiple_of(gi * SUBLANES, SUBLANES)
        hg = a_s[pl.ds(r0, SUBLANES), :] * carry + b_s[pl.ds(r0, SUBLANES), :]
        h_s[pl.ds(r0, SUBLANES), :] = hg
        return hg[SUBLANES - 1:SUBLANES, :]

    carry = lax.fori_loop(0, ng, grp, hcar_ref[...])
    hcar_ref[...] = carry
    hlast_ref[0] = carry

    y = jax.nn.gelu(gate) * h_s[...]
    out = jnp.dot(y.astype(bf16), wout_ref[...], preferred_element_type=f32)
    o_ref[0] = _res_ln(x, out, g_ref[...], b_ref[...])


def _mixer_b(x, h0, cbuf, w_in, conv_w, conv_b, wa, ba, wx, bx, lam, w_out, g, b, tb):
    B, T, _ = x.shape
    W = RNN_WIDTH
    nt = T // tb
    per_b = lambda r, c: pl.BlockSpec((1, r, c), lambda bi, ti: (bi, 0, 0))
    return pl.pallas_call(
        functools.partial(_rglru_body, tb=tb),
        grid=(B, nt),
        in_specs=[
            pl.BlockSpec((1, tb, D_MODEL), lambda bi, ti: (bi, ti, 0)),
            per_b(1, W), per_b(B_CONV - 1, W),
            _const_spec((D_MODEL, 2 * W)), _const_spec((B_CONV, W)), _const_spec((1, W)),
            _const_spec((W, W)), _const_spec((1, W)), _const_spec((W, W)), _const_spec((1, W)),
            _const_spec((1, W)), _const_spec((W, D_MODEL)), _const_spec((1, D_MODEL)), _const_spec((1, D_MODEL)),
        ],
        out_specs=[
            pl.BlockSpec((1, tb, D_MODEL), lambda bi, ti: (bi, ti, 0)),
            per_b(1, W), per_b(B_CONV - 1, W),
        ],
        out_shape=[jax.ShapeDtypeStruct((B, T, D_MODEL), f32),
                   jax.ShapeDtypeStruct((B, 1, W), f32),
                   jax.ShapeDtypeStruct((B, B_CONV - 1, W), f32)],
        scratch_shapes=[pltpu.VMEM((SUBLANES, W), f32), pltpu.VMEM((1, W), f32),
                        pltpu.VMEM((tb, W), f32), pltpu.VMEM((tb, W), f32), pltpu.VMEM((tb, W), f32)],
        compiler_params=_cparams(("arbitrary", "arbitrary")),
        name="rglru_mixer",
    )(x, h0, cbuf, w_in, conv_w, conv_b, wa, ba, wx, bx, lam, w_out, g, b)


C_HALO = 32


def _conformer_body(x_ref, cbuf_ref, win_ref, bin_ref, cw_ref, cb_ref, lg_ref, lb_ref, wout_ref, bout_ref,
                    g_ref, b_ref, o_ref, nbuf_ref, ext_ref, conv_ref, sh_ref, *, tb):
    t = pl.program_id(1)
    C = C_WIDTH
    hist = C_CONV - 1

    @pl.when(t == 0)
    def _():
        ext_ref[0:C_HALO - hist, :] = jnp.zeros((C_HALO - hist, C), f32)
        ext_ref[C_HALO - hist:C_HALO, :] = cbuf_ref[0]

    x = x_ref[0]
    z = jnp.dot(x.astype(bf16), win_ref[...], preferred_element_type=f32) + bin_ref[...]
    u = z[:, :C] * jax.nn.sigmoid(z[:, C:])
    ext_ref[C_HALO:C_HALO + tb, :] = u

    nsh = tb + C_HALO - SUBLANES
    for bres in range(1, SUBLANES):
        sh_ref[bres - 1] = ext_ref[pl.ds(bres, nsh), :]

    rt = min(tb, 128)
    for r0 in range(0, tb, rt):
        for c0 in range(0, C, LANES):
            acc = jnp.broadcast_to(cb_ref[:, c0:c0 + LANES], (rt, LANES))
            for k in range(C_CONV):
                a8, bres = divmod(k + C_HALO - hist, SUBLANES)
                if bres == 0:
                    win = ext_ref[pl.ds(r0 + SUBLANES * a8, rt), c0:c0 + LANES]
                else:
                    win = sh_ref[bres - 1, pl.ds(r0 + SUBLANES * a8, rt), c0:c0 + LANES]
                acc = acc + cw_ref[k:k + 1, c0:c0 + LANES] * win
            conv_ref[r0:r0 + rt, c0:c0 + LANES] = acc

    nbuf_ref[0] = ext_ref[tb + C_HALO - hist:tb + C_HALO, :]
    ext_ref[0:C_HALO, :] = ext_ref[tb:tb + C_HALO, :]

    c = _layer_norm(conv_ref[...], lg_ref[...], lb_ref[...])
    s = jax.nn.silu(c)
    out = jnp.dot(s.astype(bf16), wout_ref[...], preferred_element_type=f32) + bout_ref[...]
    o_ref[0] = _res_ln(x, out, g_ref[...], b_ref[...])


def _mixer_c(x, cbuf, w_in, b_in, conv_w, conv_b, ln_g, ln_b, w_out, b_out, g, b, tb):
    B, T, _ = x.shape
    C = C_WIDTH
    nt = T // tb
    return pl.pallas_call(
        functools.partial(_conformer_body, tb=tb),
        grid=(B, nt),
        in_specs=[
            pl.BlockSpec((1, tb, D_MODEL), lambda bi, ti: (bi, ti, 0)),
            pl.BlockSpec((1, C_CONV - 1, C), lambda bi, ti: (bi, 0, 0)),
            _const_spec((D_MODEL, 2 * C)), _const_spec((1, 2 * C)),
            _const_spec((C_CONV, C)), _const_spec((1, C)), _const_spec((1, C)), _const_spec((1, C)),
            _const_spec((C, D_MODEL)), _const_spec((1, D_MODEL)),
            _const_spec((1, D_MODEL)), _const_spec((1, D_MODEL)),
        ],
        out_specs=[
            pl.BlockSpec((1, tb, D_MODEL), lambda bi, ti: (bi, ti, 0)),
            pl.BlockSpec((1, C_CONV - 1, C), lambda bi, ti: (bi, 0, 0)),
        ],
        out_shape=[jax.ShapeDtypeStruct((B, T, D_MODEL), f32),
                   jax.ShapeDtypeStruct((B, C_CONV - 1, C), f32)],
        scratch_shapes=[pltpu.VMEM((tb + C_HALO, C), f32), pltpu.VMEM((tb, C), f32),
                        pltpu.VMEM((SUBLANES - 1, tb + C_HALO - SUBLANES, C), f32)],
        compiler_params=_cparams(("arbitrary", "arbitrary")),
        name="conformer_mixer",
    )(x, cbuf, w_in, b_in, conv_w, conv_b, ln_g, ln_b, w_out, b_out, g, b)


def _rwkv_front_body(x_ref, sh_ref, mu_ref, wr_ref, wk_ref, wv_ref, w0_ref, w1_ref, w2_ref,
                     a0_ref, a1_ref, a2_ref, g1_ref, g2_ref, kk_ref, ka_ref, mseg_ref,
                     r_o, lw_o, k_o, v_o, an_o, bk_o, g_o, shout_ref, car_ref, *, tb):
    t = pl.program_id(1)

    @pl.when(t == 0)
    def _():
        car_ref[...] = sh_ref[0]

    x = x_ref[0]
    (prev,) = _shift_rows(x, [car_ref[...]])
    car_ref[...] = x[tb - 1:tb, :]
    shout_ref[0] = x[tb - 1:tb, :]
    xx = prev - x
    xr, xw, xk, xv, xa, xg = (x + xx * mu_ref[n:n + 1, :] for n in range(6))
    r = _mm(xr, wr_ref[...])
    k = _mm(xk, wk_ref[...])
    v = _mm(xv, wv_ref[...])
    wl = w0_ref[...] + _mm(jnp.tanh(_mm(xw, w1_ref[...])), w2_ref[...])
    w_log = -jax.nn.softplus(-wl) - 0.5
    lw = -jnp.exp(w_log)
    a = jax.nn.sigmoid(a0_ref[...] + _mm(_mm(xa, a1_ref[...]), a2_ref[...]))
    g = _mm(jax.nn.sigmoid(_mm(xg, g1_ref[...])), g2_ref[...])
    kk = k * kk_ref[...]
    n2 = _dot2(kk * kk, mseg_ref[...])
    kk = kk / jnp.maximum(jnp.sqrt(n2), 1e-12)
    k2 = k * (1.0 + (a - 1.0) * ka_ref[...])
    outs = ((r_o, r), (lw_o, lw), (k_o, k2), (v_o, v), (an_o, -kk), (bk_o, kk * a), (g_o, g))
    for ref, val in outs:
        for p in range(D_PAIRS):
            ref[0, p] = val[:, p * LANES:(p + 1) * LANES]


def _rwkv_front(x, shift0, mu, w_r, w_k, w_v, w0, w1, w2, a0, a1, a2, g1, g2, k_k, k_a, mseg, tb):
    B, T, D = x.shape
    nt = T // tb
    pm = pl.BlockSpec((1, D_PAIRS, tb, LANES), lambda bi, ti: (bi, 0, ti, 0))
    pm_shape = jax.ShapeDtypeStruct((B, D_PAIRS, T, LANES), f32)
    vec = _const_spec((1, D))
    return pl.pallas_call(
        functools.partial(_rwkv_front_body, tb=tb),
        grid=(B, nt),
        in_specs=[
            pl.BlockSpec((1, tb, D), lambda bi, ti: (bi, ti, 0)),
            pl.BlockSpec((1, 1, D), lambda bi, ti: (bi, 0, 0)),
            _const_spec((6, D)), _const_spec((D, D)), _const_spec((D, D)), _const_spec((D, D)),
            vec, _const_spec((D, LORA_PAD)), _const_spec((LORA_PAD, D)),
            vec, _const_spec((D, LORA_PAD)), _const_spec((LORA_PAD, D)),
            _const_spec((D, LORA_PAD)), _const_spec((LORA_PAD, D)),
            vec, vec, _const_spec((D, D)),
        ],
        out_specs=[pm] * 7 + [pl.BlockSpec((1, 1, D), lambda bi, ti: (bi, 0, 0))],
        out_shape=[pm_shape] * 7 + [jax.ShapeDtypeStruct((B, 1, D), f32)],
        scratch_shapes=[pltpu.VMEM((1, D), f32)],
        compiler_params=_cparams(("arbitrary", "arbitrary")),
        name="rwkv_front",
    )(x, shift0, mu, w_r, w_k, w_v, w0, w1, w2, a0, a1, a2, g1, g2, k_k, k_a, mseg)


def _wkv_body(r_ref, lw_ref, k_ref, v_ref, an_ref, bk_ref, g_ref, s0_ref, rk_ref, gng_ref, gnb_ref,
              y_ref, sout_ref, s_ref, *, tb):
    t = pl.program_id(1)
    C = WKV_CHUNK
    P2 = 2 * C

    @pl.when(t == 0)
    def _():
        s_ref[...] = s0_ref[0]

    lane = lax.broadcasted_iota(i32, (1, LANES), 1)
    m0 = (lane < D_HEAD_DIM).astype(f32)
    m1 = 1.0 - m0
    ri = lax.broadcasted_iota(i32, (P2, P2), 0)
    ci = lax.broadcasted_iota(i32, (P2, P2), 1)
    same_head = (ri >= C) == (ci >= C)
    s_idx = ri & (C - 1)
    t_idx = ci & (C - 1)
    mask_strict = same_head & (s_idx < t_idx)
    mask_incl = same_head & (s_idx <= t_idx)
    eye = (ri == ci).astype(f32)
    seg_ones = ((ri >= D_HEAD_DIM) == (ci >= D_HEAD_DIM)).astype(f32)
    tr = lax.broadcasted_iota(i32, (C, C), 0)
    tc = lax.broadcasted_iota(i32, (C, C), 1)
    tri_incl = (tc <= tr).astype(f32)

    def stack2(z):
        return jnp.concatenate([z * m0, z * m1], axis=0)

    dims_nt = ((1,), (1,))
    dims_tn = ((0,), (0,))
    inv = 1.0 / D_HEAD_DIM
    for c in range(tb // C):
        sl = slice(c * C, (c + 1) * C)
        PR = range(D_PAIRS)
        r = [r_ref[0, p, sl, :] for p in PR]
        lw = [lw_ref[0, p, sl, :] for p in PR]
        k = [k_ref[0, p, sl, :] for p in PR]
        v = [v_ref[0, p, sl, :] for p in PR]
        an = [an_ref[0, p, sl, :] for p in PR]
        bk = [bk_ref[0, p, sl, :] for p in PR]
        cs = [_dot2l(tri_incl, lw[p]) for p in PR]
        cs_end = [cs[p][C - 1:C, :] for p in PR]
        ah = [an[p] * jnp.exp(cs[p] - lw[p]) for p in PR]
        p_inv = [jnp.exp(-cs[p]) for p in PR]
        bh = [bk[p] * p_inv[p] for p in PR]
        kh = [k[p] * p_inv[p] for p in PR]
        rh = [r[p] * jnp.exp(cs[p]) for p in PR]
        p_tail = [jnp.exp(cs_end[p] - cs[p]) for p in PR]
        X = [jnp.concatenate([stack2(bh[p]), stack2(kh[p])], axis=0) for p in PR]
        Ga = [_mm_nt(X[p], jnp.concatenate([ah[p], ah[p]], axis=0)) for p in PR]
        Gr = [_mm_nt(X[p], jnp.concatenate([rh[p], rh[p]], axis=0)) for p in PR]
        n_ab = [jnp.where(mask_strict, Ga[p][0:P2], 0.0) for p in PR]
        n_ak = [jnp.where(mask_strict, Ga[p][P2:2 * P2], 0.0) for p in PR]
        n_rb = [jnp.where(mask_incl, Gr[p][0:P2], 0.0).astype(bf16) for p in PR]
        n_rk = [jnp.where(mask_incl, Gr[p][P2:2 * P2], 0.0).astype(bf16) for p in PR]
        tt = [eye + n_ab[p] for p in PR]
        mpow = n_ab
        steps = 1
        while steps * 2 < C:
            mpow = [_mm(mpow[p], mpow[p]) for p in PR]
            tt = [tt[p] + _mm(tt[p], mpow[p]) for p in PR]
            steps *= 2
        a2 = [stack2(ah[p]) for p in PR]
        r2 = [stack2(rh[p]) for p in PR]
        v2 = [stack2(v[p]) for p in PR]
        S = [s_ref[p] for p in PR]
        rhs = [_dot3(a2[p], S[p], dims_nt) + _dot3(n_ak[p], v2[p], dims_tn) for p in PR]
        u2 = [_dot3(tt[p], rhs[p], dims_tn) for p in PR]
        for p in PR:
            bt2 = stack2(bk[p] * p_tail[p])
            kt2 = stack2(k[p] * p_tail[p])
            s_ref[p] = S[p] * jnp.exp(cs_end[p]) + _dot3(u2[p], bt2, dims_tn) + _dot3(v2[p], kt2, dims_tn)
        y2 = [_mm_nt(r2[p], S[p]) + _dg(n_rb[p], u2[p].astype(bf16), dims_tn)
              + _dg(n_rk[p], v2[p].astype(bf16), dims_tn) for p in PR]
        for p in PR:
            y = y2[p][0:C] + y2[p][C:P2]
            mu_y = _dot2(y, seg_ones) * inv
            d = y - mu_y
            var = _dot2(d * d, seg_ones) * inv
            yn = d * lax.rsqrt(var + GN_EPS) * gng_ref[p] + gnb_ref[p]
            bonus = _dot2(r[p] * k[p] * rk_ref[p], seg_ones) * v[p]
            y_ref[0, p, sl, :] = (yn + bonus) * g_ref[0, p, sl, :]

    sout_ref[0] = s_ref[...]


def _wkv(r, lw, k, v, an, bk, g, s0, r_k, gn_g, gn_b, tb):
    B, NP, T, _ = r.shape
    nt = T // tb
    pm = pl.BlockSpec((1, NP, tb, LANES), lambda bi, ti: (bi, 0, ti, 0))
    st = pl.BlockSpec((1, NP, LANES, LANES), lambda bi, ti: (bi, 0, 0, 0))
    par = _const_spec((NP, 1, LANES))
    return pl.pallas_call(
        functools.partial(_wkv_body, tb=tb),
        grid=(B, nt),
        in_specs=[pm] * 7 + [st, par, par, par],
        out_specs=[pm, st],
        out_shape=[jax.ShapeDtypeStruct((B, NP, T, LANES), f32),
                   jax.ShapeDtypeStruct((B, NP, LANES, LANES), f32)],
        scratch_shapes=[pltpu.VMEM((NP, LANES, LANES), f32)],
        compiler_params=_cparams(("arbitrary", "arbitrary")),
        name="rwkv_wkv",
    )(r, lw, k, v, an, bk, g, s0, r_k, gn_g, gn_b)


def _rwkv_out_body(y_ref, x_ref, wo_ref, g_ref, b_ref, o_ref):
    y = jnp.concatenate([y_ref[0, p] for p in range(D_PAIRS)], axis=1)
    out = jnp.dot(y.astype(bf16), wo_ref[...], preferred_element_type=f32)
    o_ref[0] = _res_ln(x_ref[0], out, g_ref[...], b_ref[...])


def _rwkv_out(y_pm, x, w_o, g, b, tb):
    B, T, D = x.shape
    nt = T // tb
    return pl.pallas_call(
        _rwkv_out_body,
        grid=(B, nt),
        in_specs=[pl.BlockSpec((1, D_PAIRS, tb, LANES), lambda bi, ti: (bi, 0, ti, 0)),
                  pl.BlockSpec((1, tb, D), lambda bi, ti: (bi, ti, 0)),
                  _const_spec((D, D)), _const_spec((1, D)), _const_spec((1, D))],
        out_specs=pl.BlockSpec((1, tb, D), lambda bi, ti: (bi, ti, 0)),
        out_shape=jax.ShapeDtypeStruct((B, T, D), f32),
        compiler_params=_cparams(("arbitrary", "arbitrary")),
        name="rwkv_out",
    )(y_pm, x, w_o, g, b)


def _rope_tables(pos, head_dim, reps):
    half = head_dim // 2
    inv_freq = jnp.exp(-math.log(ROPE_THETA) * jnp.arange(half, dtype=f32) / half)
    ang = pos.astype(f32)[:, None] * inv_freq[None, :]
    cos, sin = jnp.cos(ang), jnp.sin(ang)
    cos_t = jnp.tile(jnp.concatenate([cos, cos], axis=1), (1, reps))
    sin_t = jnp.tile(jnp.concatenate([-sin, sin], axis=1), (1, reps))
    return cos_t, sin_t


def _pad_a_w_in(w):
    q = w[:, :A_QW]
    k = w[:, A_QW:A_QW + A_KW]
    v = w[:, A_QW + A_KW:A_QW + 2 * A_KW]
    o = A_QW + 2 * A_KW
    qi = w[:, o:o + IDX_HEADS * IDX_DIM].reshape(D_MODEL, IDX_HEADS, IDX_DIM)
    qi = jnp.pad(qi, ((0, 0), (0, 0), (0, LANES - IDX_DIM))).reshape(D_MODEL, IDX_HEADS * LANES)
    o += IDX_HEADS * IDX_DIM
    ki = jnp.pad(w[:, o:o + IDX_DIM], ((0, 0), (0, LANES - IDX_DIM)))
    o += IDX_DIM
    wi = jnp.pad(w[:, o:o + IDX_HEADS], ((0, 0), (0, LANES - IDX_HEADS)))
    return jnp.concatenate([q, k, v, qi, ki, wi], axis=1).astype(bf16)


def _block_diag(w):
    n, d, _ = w.shape
    eye = jnp.eye(n, dtype=w.dtype)
    return (eye[:, None, :, None] * w[:, :, None, :]).reshape(n * d, n * d)


def _pair_state(s):
    B = s.shape[0]
    s = s.reshape(B, D_PAIRS, 2, D_HEAD_DIM, D_HEAD_DIM)
    z = jnp.zeros_like(s[:, :, 0])
    top = jnp.concatenate([s[:, :, 0], z], axis=-1)
    bot = jnp.concatenate([z, s[:, :, 1]], axis=-1)
    return jnp.concatenate([top, bot], axis=-2)


def _unpair_state(sp):
    B = sp.shape[0]
    h0 = sp[:, :, :D_HEAD_DIM, :D_HEAD_DIM]
    h1 = sp[:, :, D_HEAD_DIM:, D_HEAD_DIM:]
    return jnp.stack([h0, h1], axis=2).reshape(B, D_HEADS, D_HEAD_DIM, D_HEAD_DIM)


def _row(v):
    return v.reshape(1, -1).astype(f32)


def _mixer_a_stream(x, w_pad, w_out, g, b, pos, cache, *, tb, tq, kb_size):
    B, T, D = x.shape
    R = B * T
    c128, s128 = _rope_tables(pos, A_HEAD_DIM, 1)
    c64, s64 = _rope_tables(pos, IDX_DIM, 2)
    if tb > T:
        rep = tb // T
        c128, s128, c64, s64 = (jnp.tile(a, (rep, 1)) for a in (c128, s128, c64, s64))
    x2d = x.reshape(R, D)
    q_hm, k_o, v_o, kb, vb, qi_hm, ki_o, kib, wi = _a_project(x2d, w_pad, c128, s128, c64, s64, tb)
    kb3, vb3, kib3 = kb.reshape(B, T, A_KW), vb.reshape(B, T, A_KW), kib.reshape(B, T, LANES)
    if cache is None:
        q_pos0, l_real = 0, T
    else:
        ck, cv, cki = cache
        P = ck.shape[1]
        kb3 = jnp.concatenate([ck.reshape(B, P, A_KW).astype(bf16), kb3], axis=1)
        vb3 = jnp.concatenate([cv.reshape(B, P, A_KW).astype(bf16), vb3], axis=1)
        cki_p = jnp.pad(cki, ((0, 0), (0, 0), (0, LANES - IDX_DIM))).astype(bf16)
        kib3 = jnp.concatenate([cki_p, kib3], axis=1)
        q_pos0, l_real = P, P + T
    l_pad = -(-l_real // kb_size) * kb_size
    if l_pad > l_real:
        padk = ((0, 0), (0, l_pad - l_real), (0, 0))
        kb3, vb3, kib3 = jnp.pad(kb3, padk), jnp.pad(vb3, padk), jnp.pad(kib3, padk)
    topk = min(TOPK_MAX, l_real // 4)
    xn = _dsa_attend(q_hm, qi_hm, wi, x2d, kb3, vb3, kib3, w_out, g, b,
                     n_batch=B, tq=tq, kb_size=kb_size, q_pos0=q_pos0, l_real=l_real, topk=topk)
    return (xn.reshape(B, T, D), k_o.reshape(B, T, A_KV_HEADS, A_HEAD_DIM),
            v_o.reshape(B, T, A_KV_HEADS, A_HEAD_DIM), ki_o.reshape(B, T, IDX_DIM))


def _mixer_d_stream(x, wkv0, shift0, prm, g, b, *, tb, tb_wkv):
    B, T, D = x.shape
    outs = _rwkv_front(x, shift0.reshape(B, 1, D), prm["mu"], prm["w_r"], prm["w_k"], prm["w_v"], prm["w0"],
                       prm["w1"], prm["w2"], prm["a0"], prm["a1"], prm["a2"], prm["g1"], prm["g2"],
                       prm["k_k"], prm["k_a"], prm["mseg"], tb)
    seq, shift_out = outs[:7], outs[7]
    t_pad = -(-T // tb_wkv) * tb_wkv
    if t_pad > T:
        seq = [jnp.pad(a, ((0, 0), (0, 0), (0, t_pad - T), (0, 0))) for a in seq]
    y_pm, s_out = _wkv(*seq, _pair_state(wkv0), prm["r_k"], prm["gn_g"], prm["gn_b"], tb_wkv)
    if t_pad > T:
        y_pm = y_pm[:, :, :T]
    xn = _rwkv_out(y_pm, x, prm["w_o"], g, b, tb)
    return xn, _unpair_state(s_out), shift_out.reshape(B, D)


def kernel(x_prompt, x_sample, cache_a_k, cache_a_v, cache_a_kidx, state_b_h, state_b_conv, state_c_conv,
           state_d_wkv, state_d_shift, state_f_conv, ln_g, ln_b, a_w_in, a_w_out,
           b_w_in, b_conv_w, b_conv_b, b_gate_a_w, b_gate_a_b, b_gate_x_w, b_gate_x_b, b_lambda, b_w_out,
           c_w_in, c_b_in, c_conv_w, c_conv_b, c_ln_g, c_ln_b, c_w_out, c_b_out,
           d_mu, d_w_r, d_w_k, d_w_v, d_w_o, d_w0, d_w1, d_w2, d_a0, d_a1, d_a2, d_g1, d_g2,
           d_k_k, d_k_a, d_r_k, d_gn_g, d_gn_b, f_w_up, f_conv_w, f_conv_b, f_w_down):
    xp, xs = x_prompt, x_sample
    BP, TP, D = xp.shape
    BS, TS, _ = xs.shape
    tb_p = 256
    tb_s = TS
    outs = {k: [] for k in ("ak_p", "ak_s", "av_p", "av_s", "aki_p", "aki_s", "bh_p", "bh_s", "bc_p", "bc_s",
                            "cc_p", "cc_s", "dw_p", "dw_s", "dsh_p", "dsh_s", "fc_p", "fc_s")}
    for i in range(DEPTH):
        m, j = i % 4, i // 4
        g0, b0 = _row(ln_g[i, 0]), _row(ln_b[i, 0])
        g1, b1 = _row(ln_g[i, 1]), _row(ln_b[i, 1])
        if m == 0:
            w_pad = _pad_a_w_in(a_w_in[j])
            w_out = a_w_out[j].astype(bf16)
            xp, kp, vp, kip = _mixer_a_stream(xp, w_pad, w_out, g0, b0, jnp.arange(TP, dtype=i32), None,
                                              tb=512, tq=128, kb_size=512)
            P = cache_a_k.shape[2]
            xs, ks, vs, kis = _mixer_a_stream(xs, w_pad, w_out, g0, b0, P + jnp.arange(TS, dtype=i32),
                                              (cache_a_k[j], cache_a_v[j], cache_a_kidx[j]),
                                              tb=BS * TS, tq=TS, kb_size=256)
            outs["ak_p"].append(kp); outs["ak_s"].append(ks)
            outs["av_p"].append(vp); outs["av_s"].append(vs)
            outs["aki_p"].append(kip); outs["aki_s"].append(kis)
        elif m == 1:
            W = RNN_WIDTH
            args = (b_w_in[j].astype(bf16), b_conv_w[j], _row(b_conv_b[j]),
                    _block_diag(b_gate_a_w[j]).astype(bf16), _row(b_gate_a_b[j]),
                    _block_diag(b_gate_x_w[j]).astype(bf16), _row(b_gate_x_b[j]),
                    _row(b_lambda[j]), b_w_out[j].astype(bf16), g0, b0)
            xp, hp, cp = _mixer_b(xp, jnp.zeros((BP, 1, W), f32), jnp.zeros((BP, B_CONV - 1, W), f32), *args, tb_p)
            xs, hs, cs = _mixer_b(xs, state_b_h[j].reshape(BS, 1, W), state_b_conv[j], *args, tb_s)
            outs["bh_p"].append(hp.reshape(BP, W)); outs["bh_s"].append(hs.reshape(BS, W))
            outs["bc_p"].append(cp); outs["bc_s"].append(cs)
        elif m == 2:
            args = (c_w_in[j].astype(bf16), _row(c_b_in[j]), c_conv_w[j], _row(c_conv_b[j]),
                    _row(c_ln_g[j]), _row(c_ln_b[j]), c_w_out[j].astype(bf16), _row(c_b_out[j]), g0, b0)
            xp, cp = _mixer_c(xp, jnp.zeros((BP, C_CONV - 1, C_WIDTH), f32), *args, tb_p)
            xs, cs = _mixer_c(xs, state_c_conv[j], *args, tb_s)
            outs["cc_p"].append(cp); outs["cc_s"].append(cs)
        else:
            lp = ((0, 0), (0, LORA_PAD - d_w1.shape[2]))
            lq = ((0, LORA_PAD - d_w1.shape[2]), (0, 0))
            gp = ((0, 0), (0, LORA_PAD - d_g1.shape[2]))
            gq = ((0, LORA_PAD - d_g1.shape[2]), (0, 0))
            pairs = lambda v: v.reshape(D_PAIRS, 1, LANES).astype(f32)
            prm = dict(
                mu=d_mu[j], w_r=d_w_r[j].astype(bf16), w_k=d_w_k[j].astype(bf16), w_v=d_w_v[j].astype(bf16),
                w_o=d_w_o[j].astype(bf16), w0=_row(d_w0[j]),
                w1=jnp.pad(d_w1[j], lp).astype(bf16), w2=jnp.pad(d_w2[j], lq).astype(bf16),
                a0=_row(d_a0[j]), a1=jnp.pad(d_a1[j], lp).astype(bf16), a2=jnp.pad(d_a2[j], lq).astype(bf16),
                g1=jnp.pad(d_g1[j], gp).astype(bf16), g2=jnp.pad(d_g2[j], gq).astype(bf16),
                k_k=_row(d_k_k[j]), k_a=_row(d_k_a[j]),
                mseg=_block_diag(jnp.ones((D_HEADS, D_HEAD_DIM, D_HEAD_DIM), bf16)),
                r_k=pairs(d_r_k[j]), gn_g=pairs(d_gn_g[j]), gn_b=pairs(d_gn_b[j]))
            xp, sp, shp = _mixer_d_stream(xp, jnp.zeros((BP, D_HEADS, D_HEAD_DIM, D_HEAD_DIM), f32),
                                          jnp.zeros((BP, D), f32), prm, g0, b0, tb=tb_p, tb_wkv=WKV_CHUNK)
            xs, ss, shs = _mixer_d_stream(xs, state_d_wkv[j], state_d_shift[j], prm, g0, b0,
                                          tb=tb_s, tb_wkv=WKV_CHUNK)
            outs["dw_p"].append(sp); outs["dw_s"].append(ss)
            outs["dsh_p"].append(shp); outs["dsh_s"].append(shs)
        fargs = (f_w_up[i].astype(bf16), f_conv_w[i], _row(f_conv_b[i]), f_w_down[i].astype(bf16), g1, b1)
        xp, fbp = _conv_ffn(xp, jnp.zeros((BP, F_CONV - 1, 2 * D_FF), f32), *fargs, tb_p)
        xs, fbs = _conv_ffn(xs, state_f_conv[i], *fargs, tb_s)
        outs["fc_p"].append(fbp); outs["fc_s"].append(fbs)
    st = lambda k: jnp.stack(outs[k])
    return (xp, xs,
            st("ak_p"), st("ak_s"), st("av_p"), st("av_s"), st("aki_p"), st("aki_s"),
            st("bh_p"), st("bh_s"), st("bc_p"), st("bc_s"), st("cc_p"), st("cc_s"),
            st("dw_p"), st("dw_s"), st("dsh_p"), st("dsh_s"), st("fc_p"), st("fc_s"))
```

```python
import functools
import math

import jax
import jax.numpy as jnp
from jax import lax
from jax.experimental import pallas as pl
from jax.experimental.pallas import tpu as pltpu

f32 = jnp.float32
bf16 = jnp.bfloat16
i32 = jnp.int32

D_MODEL = 1024
DEPTH = 4
CHUNK = 64
ALPHA = (2.0 * DEPTH) ** 0.25
LN_EPS = 1e-5
NEG_INF = -1e30

A_HEADS = 8
A_KV_HEADS = 2
A_HEAD_DIM = 128
A_GROUP = A_HEADS // A_KV_HEADS
IDX_HEADS = 8
IDX_DIM = 64
IDX_SCALE = (IDX_DIM ** -0.5) * (IDX_HEADS ** -0.5)
TOPK_MAX = 256
ROPE_THETA = 10000.0

RNN_WIDTH = 1408
LRU_BLOCKS = 8
LRU_BLOCK_DIM = RNN_WIDTH // LRU_BLOCKS
B_CONV = 4
LRU_C = 8.0

C_WIDTH = 1024
C_CONV = 31

D_HEAD_DIM = 64
D_HEADS = D_MODEL // D_HEAD_DIM
D_PAIRS = D_HEADS // 2
GN_EPS = 64e-5
LORA_PAD = 128

D_FF = 2816
F_CONV = 3

LANES = 128
SUBLANES = 8
VMEM_LIMIT_BYTES = 60 * 1024 * 1024
WKV_CHUNK = 64

INT_MIN = -(2 ** 31)
INT_MAX = 2 ** 31 - 1


def _cparams(sem):
    return pltpu.CompilerParams(dimension_semantics=sem, vmem_limit_bytes=VMEM_LIMIT_BYTES)


def _const_spec(shape):
    nd = len(shape)
    return pl.BlockSpec(shape, lambda *_: (0,) * nd, pipeline_mode=pl.Buffered(1))


def _mm(a, b):
    return jnp.dot(a.astype(bf16), b.astype(bf16), preferred_element_type=f32)


def _split(a):
    hi = a.astype(bf16)
    lo = (a - hi.astype(f32)).astype(bf16)
    return hi, lo


def _dg(a, b, dims):
    return lax.dot_general(a, b, (dims, ((), ())), preferred_element_type=f32)


def _mm_nt(a, b):
    return _dg(a.astype(bf16), b.astype(bf16), ((1,), (1,)))


def _dot3(a, b, dims=((1,), (0,))):
    ah, al = _split(a)
    bh, bl = _split(b)
    return _dg(ah, bh, dims) + (_dg(ah, bl, dims) + _dg(al, bh, dims))


def _dot2(a, b_exact, dims=((1,), (0,))):
    ah, al = _split(a)
    bb = b_exact.astype(bf16)
    return _dg(ah, bb, dims) + _dg(al, bb, dims)


def _dot2l(a_exact, b, dims=((1,), (0,))):
    bh, bl = _split(b)
    aa = a_exact.astype(bf16)
    return _dg(aa, bh, dims) + _dg(aa, bl, dims)


def _layer_norm(z, g, b, eps=LN_EPS):
    mu = jnp.mean(z, axis=-1, keepdims=True)
    zc = z - mu
    var = jnp.mean(zc * zc, axis=-1, keepdims=True)
    return zc * lax.rsqrt(var + eps) * g + b


def _res_ln(x, y, g, b):
    return _layer_norm(ALPHA * x + y, g, b)


def _shift_rows(h, carry_rows):
    n = len(carry_rows)
    row = lax.broadcasted_iota(i32, (SUBLANES, h.shape[1]), 0)
    out = []
    for k in range(1, n + 1):
        hk = pltpu.roll(h, k, axis=0)
        top = hk[0:SUBLANES]
        for j in range(k):
            top = jnp.where(row == j, carry_rows[n - (k - j)], top)
        out.append(jnp.concatenate([top, hk[SUBLANES:]], axis=0) if h.shape[0] > SUBLANES else top)
    return out


FFN_TN = 256


def _ffn_body(x_ref, buf_ref, wup_ref, cw_ref, cb_ref, wdn_ref, g_ref, b_ref,
              o_ref, nbuf_ref, carry_ref, *, tb):
    t = pl.program_id(1)

    @pl.when(t == 0)
    def _():
        carry_ref[6:8, :] = buf_ref[0]

    x = x_ref[0]
    xb = x.astype(bf16)
    nj = D_FF // FFN_TN

    def up(j):
        return [jnp.dot(xb, wup_ref[:, part * D_FF + j * FFN_TN: part * D_FF + (j + 1) * FFN_TN],
                        preferred_element_type=f32) for part in range(2)]

    acc = jnp.zeros((tb, D_MODEL), f32)
    hs_next = up(0)
    for j in range(nj):
        hs = hs_next
        if j + 1 < nj:
            hs_next = up(j + 1)
        us = []
        for part in range(2):
            c0 = part * D_FF + j * FFN_TN
            h = hs[part]
            cm2 = carry_ref[6:7, c0:c0 + FFN_TN]
            cm1 = carry_ref[7:8, c0:c0 + FFN_TN]
            h1, h2 = _shift_rows(h, [cm2, cm1])
            u = (cw_ref[0:1, c0:c0 + FFN_TN] * h2 + cw_ref[1:2, c0:c0 + FFN_TN] * h1
                 + cw_ref[2:3, c0:c0 + FFN_TN] * h + cb_ref[:, c0:c0 + FFN_TN])
            carry_ref[6:8, c0:c0 + FFN_TN] = h[tb - 2:tb, :]
            us.append(u)
        act = jax.nn.silu(us[0]) * us[1]
        acc = acc + jnp.dot(act.astype(bf16), wdn_ref[j * FFN_TN:(j + 1) * FFN_TN, :],
                            preferred_element_type=f32)
    nbuf_ref[0] = carry_ref[6:8, :]
    o_ref[0] = _res_ln(x, acc, g_ref[...], b_ref[...])


def _conv_ffn(x, buf, w_up, conv_w, conv_b, w_down, g, b, tb):
    B, T, _ = x.shape
    nt = T // tb
    return pl.pallas_call(
        functools.partial(_ffn_body, tb=tb),
        grid=(B, nt),
        in_specs=[
            pl.BlockSpec((1, tb, D_MODEL), lambda bi, ti: (bi, ti, 0)),
            pl.BlockSpec((1, F_CONV - 1, 2 * D_FF), lambda bi, ti: (bi, 0, 0)),
            _const_spec((D_MODEL, 2 * D_FF)),
            _const_spec((F_CONV, 2 * D_FF)),
            _const_spec((1, 2 * D_FF)),
            _const_spec((D_FF, D_MODEL)),
            _const_spec((1, D_MODEL)),
            _const_spec((1, D_MODEL)),
        ],
        out_specs=[
            pl.BlockSpec((1, tb, D_MODEL), lambda bi, ti: (bi, ti, 0)),
            pl.BlockSpec((1, F_CONV - 1, 2 * D_FF), lambda bi, ti: (bi, 0, 0)),
        ],
        out_shape=[jax.ShapeDtypeStruct((B, T, D_MODEL), f32),
                   jax.ShapeDtypeStruct((B, F_CONV - 1, 2 * D_FF), f32)],
        scratch_shapes=[pltpu.VMEM((SUBLANES, 2 * D_FF), f32)],
        compiler_params=_cparams(("arbitrary", "arbitrary")),
        name="conv_ffn",
    )(x, buf, w_up, conv_w, conv_b, w_down, g, b)


A_QW = A_HEADS * A_HEAD_DIM
A_KW = A_KV_HEADS * A_HEAD_DIM
A_PROJ_PAD = A_QW + 2 * A_KW + IDX_HEADS * LANES + LANES + LANES
LOG2_E = math.log2(math.e)
TOPM = 12
TOPM_ROWS = 16
ATTN_ROW_CHUNK = 32


def _rope128(z, cos, sin):
    return z * cos + pltpu.roll(z, A_HEAD_DIM // 2, axis=1) * sin


def _rope64(z, cos, sin):
    lane = lax.broadcasted_iota(i32, z.shape, 1)
    rot = jnp.where((lane & 32) == 0, pltpu.roll(z, LANES - 32, axis=1), pltpu.roll(z, 32, axis=1))
    return z * cos + rot * sin


def _aproj_body(x_ref, w_ref, c128_ref, s128_ref, c64_ref, s64_ref,
                q_ref, k_ref, v_ref, kb_ref, vb_ref, qi_ref, ki_ref, kib_ref, wi_ref):
    xb = x_ref[...].astype(bf16)
    c128, s128 = c128_ref[...], s128_ref[...]
    c64, s64 = c64_ref[...], s64_ref[...]
    qscale = A_HEAD_DIM ** -0.5 * LOG2_E

    def proj2(c0):
        z = jnp.dot(xb, w_ref[:, c0:c0 + 2 * LANES], preferred_element_type=f32)
        return z[:, :LANES], z[:, LANES:]

    for h in range(0, A_HEADS, 2):
        for hh, z in zip((h, h + 1), proj2(h * LANES)):
            q_ref[hh] = (_rope128(z, c128, s128) * qscale).astype(bf16)
    for h, z in enumerate(proj2(A_QW)):
        kr = _rope128(z, c128, s128)
        k_ref[:, h * LANES:(h + 1) * LANES] = kr
        kb_ref[:, h * LANES:(h + 1) * LANES] = kr.astype(bf16)
    for h, vv in enumerate(proj2(A_QW + A_KW)):
        v_ref[:, h * LANES:(h + 1) * LANES] = vv
        vb_ref[:, h * LANES:(h + 1) * LANES] = vv.astype(bf16)
    base = A_QW + 2 * A_KW
    for h in range(0, IDX_HEADS, 2):
        for hh, z in zip((h, h + 1), proj2(base + h * LANES)):
            qi_ref[hh] = _rope64(z, c64, s64).astype(bf16)
    base = base + IDX_HEADS * LANES
    zk, zw = proj2(base)
    kir = _rope64(zk, c64, s64)
    ki_ref[...] = kir[:, :IDX_DIM]
    kib_ref[...] = kir.astype(bf16)
    wi_ref[...] = zw * IDX_SCALE


def _a_project(x2d, w_pad, c128, s128, c64, s64, tb):
    R = x2d.shape[0]
    nblk = R // tb
    ntab = c128.shape[0] // tb
    tab = pl.BlockSpec((tb, LANES), lambda i: (i % ntab, 0))
    row = lambda w: pl.BlockSpec((tb, w), lambda i: (i, 0))
    hm = pl.BlockSpec((A_HEADS, tb, LANES), lambda i: (0, i, 0))
    return pl.pallas_call(
        _aproj_body,
        grid=(nblk,),
        in_specs=[row(D_MODEL), _const_spec((D_MODEL, A_PROJ_PAD)), tab, tab, tab, tab],
        out_specs=[hm, row(A_KW), row(A_KW), row(A_KW), row(A_KW), hm, row(IDX_DIM), row(LANES), row(LANES)],
        out_shape=[
            jax.ShapeDtypeStruct((A_HEADS, R, LANES), bf16),
            jax.ShapeDtypeStruct((R, A_KW), f32),
            jax.ShapeDtypeStruct((R, A_KW), f32),
            jax.ShapeDtypeStruct((R, A_KW), bf16),
            jax.ShapeDtypeStruct((R, A_KW), bf16),
            jax.ShapeDtypeStruct((IDX_HEADS, R, LANES), bf16),
            jax.ShapeDtypeStruct((R, IDX_DIM), f32),
            jax.ShapeDtypeStruct((R, LANES), bf16),
            jax.ShapeDtypeStruct((R, LANES), f32),
        ],
        compiler_params=_cparams(("arbitrary",)),
        name="dsa_project",
    )(x2d, w_pad, c128, s128, c64, s64)


def _mono_key(s):
    i = lax.bitcast_convert_type(s, i32)
    return i ^ ((i >> 31) & jnp.int32(INT_MAX))


def _key_to_f32(k):
    return lax.bitcast_convert_type(k ^ ((k >> 31) & jnp.int32(INT_MAX)), f32)


KEY_PAD = -(2 ** 31) + 0x7FFFFF


def _dsa_body(q_ref, qi_ref, wi_ref, x_ref, k_hbm, v_hbm, ki_hbm, wo_ref, g_ref, b_ref,
              o_ref, k_vm, v_vm, ki_vm, keys_ref, cand_ref, candt_ref, m_ref, l_ref, alpha_ref, acc_ref, s_ref, p_ref,
              bias_ref, sem,
              *, tq, kb_size, nkb_total, q_pos0, l_real, topk):
    bi = pl.program_id(0)
    qi_blk = pl.program_id(1)

    @pl.when(qi_blk == 0)
    def _():
        copies = [pltpu.make_async_copy(k_hbm.at[bi], k_vm, sem.at[0]),
                  pltpu.make_async_copy(v_hbm.at[bi], v_vm, sem.at[1]),
                  pltpu.make_async_copy(ki_hbm.at[bi], ki_vm, sem.at[2])]
        for c in copies:
            c.start()
        for c in copies:
            c.wait()

    q0 = q_pos0 + qi_blk * tq
    last_chunk_end = ((q0 + tq - 1) // CHUNK + 1) * CHUNK
    nkb = jnp.minimum((last_chunk_end + kb_size - 1) // kb_size, nkb_total)

    qpos = q0 + lax.broadcasted_iota(i32, (tq, 1), 0)
    qchunk = qpos >> 6
    lane_k = lax.broadcasted_iota(i32, (1, kb_size), 1)

    def valid_mask(kb):
        kpos = kb * kb_size + lane_k
        ok = (kpos >> 6) <= qchunk
        return ok, kpos

    qi_all = qi_ref[...].reshape(IDX_HEADS * tq, LANES)
    wi = wi_ref[...]
    wcols = [wi[:, h:h + 1] for h in range(IDX_HEADS)]

    def score_block(kb):
        off = pl.multiple_of(kb * kb_size, kb_size)
        kiblk = ki_vm[pl.ds(off, kb_size), :]
        rel = _dg(qi_all, kiblk, ((1,), (1,))).reshape(IDX_HEADS, tq, kb_size)
        score = jnp.maximum(rel[0], 0.0) * wcols[0]
        for h in range(1, IDX_HEADS):
            score = score + jnp.maximum(rel[h], 0.0) * wcols[h]
        adm, kpos = valid_mask(kb)
        score = jnp.where(score == 0.0, 0.0, score)
        key = _mono_key(jnp.where(adm, score, NEG_INF))
        if l_real < nkb_total * kb_size:
            key = jnp.where(kpos < l_real, key, jnp.int32(KEY_PAD))
        keys_ref[kb] = key

    def score_pair(i, carry):
        score_block(2 * i)
        score_block(jnp.minimum(2 * i + 1, nkb - 1))
        return carry

    lax.fori_loop(0, (nkb + 1) // 2, score_pair, 0)

    nchunk = kb_size // LANES

    def count_where(pred):
        def body(kb, accv):
            for c in range(nchunk):
                blk = keys_ref[kb, :, c * LANES:(c + 1) * LANES]
                accv = accv + pred(blk, kb * kb_size + c * LANES).astype(i32)
            return accv
        accv = lax.fori_loop(0, nkb, body, jnp.zeros((tq, LANES), i32))
        return jnp.sum(accv, axis=1, keepdims=True)

    def bisect(count_ge):
        def bit_step(it, carry):
            p, cnt_p = carry
            bit = 31 - it
            cand = p | (jnp.int32(1) << bit)
            cnt = count_ge(jnp.broadcast_to(cand ^ jnp.int32(INT_MIN), (tq, LANES)))
            ok = cnt >= topk
            return jnp.where(ok, cand, p), jnp.where(ok, cnt, cnt_p)
        return lax.fori_loop(0, 32, bit_step, (jnp.zeros((tq, 1), i32), jnp.zeros((tq, 1), i32)))

    def lane_topm(rg, carry):
        r0 = pl.multiple_of(rg * TOPM_ROWS, TOPM_ROWS)

        def body(kb, lists):
            for c in range(nchunk):
                x = _key_to_f32(keys_ref[kb, pl.ds(r0, TOPM_ROWS), c * LANES:(c + 1) * LANES])
                new = []
                for li in lists:
                    new.append(jnp.maximum(li, x))
                    x = jnp.minimum(li, x)
                lists = tuple(new)
            return lists

        init = tuple(jnp.full((TOPM_ROWS, LANES), -jnp.inf, f32) for _ in range(TOPM))
        lists = lax.fori_loop(0, nkb, body, init)
        for i in range(TOPM):
            cand_ref[i, pl.ds(r0, TOPM_ROWS), :] = _mono_key(lists[i])
        return carry

    lax.fori_loop(0, tq // TOPM_ROWS, lane_topm, 0)

    def count_cand_ge(thr_b):
        accv = (cand_ref[0] >= thr_b).astype(i32)
        for i in range(1, TOPM):
            accv = accv + (cand_ref[i] >= thr_b).astype(i32)
        return jnp.sum(accv, axis=1, keepdims=True)

    def bisect_rows_on_lanes():
        for i in range(TOPM):
            candt_ref[i] = cand_ref[i].T

        def bit_step(it, carry):
            p, cnt_p = carry
            bit = 31 - it
            cand = p | (jnp.int32(1) << bit)
            thr_r = cand ^ jnp.int32(INT_MIN)
            accv = (candt_ref[0] >= thr_r).astype(i32)
            for i in range(1, TOPM):
                accv = accv + (candt_ref[i] >= thr_r).astype(i32)
            cnt = jnp.sum(accv, axis=0, keepdims=True)
            ok = cnt >= topk
            return jnp.where(ok, cand, p), jnp.where(ok, cnt, cnt_p)

        p_r, cnt_r = lax.fori_loop(0, 32, bit_step, (jnp.zeros((1, tq), i32), jnp.zeros((1, tq), i32)))
        eye = lax.broadcasted_iota(i32, (tq, tq), 0) == lax.broadcasted_iota(i32, (tq, tq), 1)

        def to_col(row):
            return jnp.sum(jnp.where(eye, jnp.broadcast_to(row, (tq, tq)), 0), axis=1, keepdims=True)

        return to_col(p_r), to_col(cnt_r)

    p_c, cnt_c = bisect_rows_on_lanes() if tq == LANES else bisect(count_cand_ge)
    thr_c = jnp.broadcast_to(p_c ^ jnp.int32(INT_MIN), (tq, LANES))
    min_kept = cand_ref[TOPM - 1]
    covered = (min_kept < thr_c) | (min_kept == jnp.int32(KEY_PAD))
    all_covered = jnp.min(covered.astype(f32)) > 0.0
    p_fin, cnt_ge = lax.cond(all_covered, lambda: (p_c, cnt_c),
                             lambda: bisect(lambda thr_b: count_where(lambda blk, _: blk >= thr_b)))
    thr = p_fin ^ jnp.int32(INT_MIN)
    thr_b = jnp.broadcast_to(thr, (tq, LANES))
    has_excess = cnt_ge > topk
    lane_i = lax.broadcasted_iota(i32, (tq, LANES), 1)

    def tie_limit():
        cnt_gt = count_where(lambda blk, _: blk > thr_b)
        need = topk - cnt_gt

        def idx_step(it, jj):
            bit = 14 - it
            cand = jj | (jnp.int32(1) << bit)
            cand_b = jnp.broadcast_to(cand, (tq, LANES))
            g = count_where(lambda blk, base: (blk == thr_b) & ((lane_i + base) < cand_b))
            return jnp.where(g < need, cand, jj)
        jj = lax.fori_loop(0, 15, idx_step, jnp.zeros((tq, 1), i32))
        return jnp.where(has_excess, jj, jnp.int32(INT_MAX))

    any_excess = jnp.max(has_excess.astype(f32)) > 0.0
    tie_j = lax.cond(any_excess, tie_limit, lambda: jnp.full((tq, 1), INT_MAX, i32))

    m_ref[...] = jnp.full(m_ref.shape, NEG_INF, f32)
    l_ref[...] = jnp.zeros(l_ref.shape, f32)
    acc_ref[...] = jnp.zeros(acc_ref.shape, f32)
    gq = A_GROUP * tq
    rc = min(tq, ATTN_ROW_CHUNK)

    def attn_block(kb, carry):
        off = pl.multiple_of(kb * kb_size, kb_size)
        key = keys_ref[kb]
        adm, kpos = valid_mask(kb)
        sel = (key > thr) | ((key == thr) & (kpos <= tie_j))
        mask = sel & adm
        if l_real < nkb_total * kb_size:
            mask = mask & (kpos < l_real)
        bias_ref[...] = jnp.where(mask, 0.0, NEG_INF)
        for g in range(A_KV_HEADS):
            qg = q_ref[g * A_GROUP:(g + 1) * A_GROUP].reshape(gq, LANES)
            kblk = k_vm[pl.ds(off, kb_size), g * LANES:(g + 1) * LANES]
            s_ref[g] = _dg(qg, kblk, ((1,), (1,)))
        for g in range(A_KV_HEADS):
            for r0 in range(0, gq, rc):
                qr0 = r0 % tq
                m_old = m_ref[g, r0:r0 + rc, :]
                parts = []
                mx = None
                for c in range(nchunk):
                    cs_ = slice(c * LANES, (c + 1) * LANES)
                    sc = s_ref[g, r0:r0 + rc, cs_] + bias_ref[qr0:qr0 + rc, cs_]
                    parts.append(sc)
                    mx = sc if mx is None else jnp.maximum(mx, sc)
                m_new = jnp.maximum(m_old, jnp.max(mx, axis=1, keepdims=True))
                alpha = jnp.exp2(m_old - m_new)
                psum = None
                for c in range(nchunk):
                    p = jnp.exp2(parts[c] - m_new)
                    p_ref[g, r0:r0 + rc, c * LANES:(c + 1) * LANES] = p.astype(bf16)
                    psum = p if psum is None else psum + p
                l_ref[g, r0:r0 + rc, :] = alpha * l_ref[g, r0:r0 + rc, :] + jnp.sum(psum, axis=1, keepdims=True)
                m_ref[g, r0:r0 + rc, :] = m_new
                alpha_ref[g, r0:r0 + rc, :] = alpha
            vblk = v_vm[pl.ds(off, kb_size), g * LANES:(g + 1) * LANES]
            acc_ref[g] = alpha_ref[g] * acc_ref[g] + jnp.dot(p_ref[g], vblk, preferred_element_type=f32)
        return carry

    lax.fori_loop(0, nkb, attn_block, 0)

    heads = []
    for g in range(A_KV_HEADS):
        og = acc_ref[g] / l_ref[g]
        for j in range(A_GROUP):
            heads.append(og[j * tq:(j + 1) * tq])
    o = jnp.concatenate(heads, axis=1)
    y = jnp.dot(o.astype(bf16), wo_ref[...], preferred_element_type=f32)
    o_ref[...] = _res_ln(x_ref[...], y, g_ref[...], b_ref[...])


def _dsa_attend(q_hm, qi_hm, wi, x2d, kb_all, vb_all, kib_all, w_out, g, b,
                *, n_batch, tq, kb_size, q_pos0, l_real, topk):
    R = x2d.shape[0]
    L = kb_all.shape[1]
    nq = R // (n_batch * tq)
    nkb_total = L // kb_size
    body = functools.partial(_dsa_body, tq=tq, kb_size=kb_size, nkb_total=nkb_total,
                             q_pos0=q_pos0, l_real=l_real, topk=topk)
    hm = pl.BlockSpec((A_HEADS, tq, LANES), lambda bi, qi: (0, bi * nq + qi, 0))
    row = lambda w: pl.BlockSpec((tq, w), lambda bi, qi: (bi * nq + qi, 0))
    anyspec = pl.BlockSpec(memory_space=pl.ANY)
    return pl.pallas_call(
        body,
        grid=(n_batch, nq),
        in_specs=[hm, hm, row(LANES), row(D_MODEL), anyspec, anyspec, anyspec,
                  _const_spec((A_QW, D_MODEL)), _const_spec((1, D_MODEL)), _const_spec((1, D_MODEL))],
        out_specs=row(D_MODEL),
        out_shape=jax.ShapeDtypeStruct((R, D_MODEL), f32),
        scratch_shapes=[
            pltpu.VMEM((L, A_KW), bf16),
            pltpu.VMEM((L, A_KW), bf16),
            pltpu.VMEM((L, LANES), bf16),
            pltpu.VMEM((nkb_total, tq, kb_size), i32),
            pltpu.VMEM((TOPM, tq, LANES), i32),
            pltpu.VMEM((TOPM, LANES, tq), i32),
            pltpu.VMEM((A_KV_HEADS, A_GROUP * tq, LANES), f32),
            pltpu.VMEM((A_KV_HEADS, A_GROUP * tq, LANES), f32),
            pltpu.VMEM((A_KV_HEADS, A_GROUP * tq, LANES), f32),
            pltpu.VMEM((A_KV_HEADS, A_GROUP * tq, LANES), f32),
            pltpu.VMEM((A_KV_HEADS, A_GROUP * tq, kb_size), f32),
            pltpu.VMEM((A_KV_HEADS, A_GROUP * tq, kb_size), bf16),
            pltpu.VMEM((tq, kb_size), f32),
            pltpu.SemaphoreType.DMA((3,)),
        ],
        compiler_params=_cparams(("arbitrary", "arbitrary")),
        name="dsa_attend",
    )(q_hm, qi_hm, wi, x2d, kb_all, vb_all, kib_all, w_out, g, b)


def _rglru_body(x_ref, h0_ref, cbuf_ref, win_ref, cw_ref, cb_ref, wa_ref, ba_ref, wx_ref, bx_ref,
                lam_ref, wout_ref, g_ref, b_ref,
                o_ref, hlast_ref, nbuf_ref,
                cconv_ref, hcar_ref, a_s, b_s, h_s, *, tb):
    t = pl.program_id(1)
    W = RNN_WIDTH

    @pl.when(t == 0)
    def _():
        cconv_ref[5:8, :] = cbuf_ref[0]
        hcar_ref[...] = h0_ref[0]

    x = x_ref[0]
    z = jnp.dot(x.astype(bf16), win_ref[...], preferred_element_type=f32)
    gate = z[:, :W]
    rec = z[:, W:]
    c3, c2, c1 = cconv_ref[5:6, :], cconv_ref[6:7, :], cconv_ref[7:8, :]
    r1, r2, r3 = _shift_rows(rec, [c3, c2, c1])
    u = (cw_ref[0:1, :] * r3 + cw_ref[1:2, :] * r2 + cw_ref[2:3, :] * r1 + cw_ref[3:4, :] * rec + cb_ref[...])
    cconv_ref[5:8, :] = rec[tb - 3:tb, :]
    nbuf_ref[0] = rec[tb - 3:tb, :]

    ub = u.astype(bf16)
    r = jax.nn.sigmoid(jnp.dot(ub, wa_ref[...], preferred_element_type=f32) + ba_ref[...])
    ig = jax.nn.sigmoid(jnp.dot(ub, wx_ref[...], preferred_element_type=f32) + bx_ref[...])
    log_a = (-LRU_C) * r * jax.nn.softplus(-lam_ref[...])
    a = jnp.exp(log_a)
    one_m_a2 = -jnp.tanh(log_a) * (a * a + 1.0)
    bv = jnp.sqrt(one_m_a2) * (ig * u)

    ng = tb // SUBLANES
    a3 = a.reshape(ng, SUBLANES, W)
    b3 = bv.reshape(ng, SUBLANES, W)
    sub = lax.broadcasted_iota(i32, (ng, SUBLANES, W), 1)
    off = 1
    while off < SUBLANES:
        m = sub >= off
        b3 = jnp.where(m, a3 * pltpu.roll(b3, off, axis=1) + b3, b3)
        a3 = jnp.where(m, a3 * pltpu.roll(a3, off, axis=1), a3)
        off *= 2
    a_s[...] = a3.reshape(tb, W)
    b_s[...] = b3.reshape(tb, W)

    def grp(gi, carry):
        r0 = pl.multiple_of(gi * SUBLANES, SUBLANES)
        hg = a_s[pl.ds(r0, SUBLANES), :] * carry + b_s[pl.ds(r0, SUBLANES), :]
        h_s[pl.ds(r0, SUBLANES), :] = hg
        return hg[SUBLANES - 1:SUBLANES, :]

    carry = lax.fori_loop(0, ng, grp, hcar_ref[...])
    hcar_ref[...] = carry
    hlast_ref[0] = carry

    y = jax.nn.gelu(gate) * h_s[...]
    out = jnp.dot(y.astype(bf16), wout_ref[...], preferred_element_type=f32)
    o_ref[0] = _res_ln(x, out, g_ref[...], b_ref[...])


def _mixer_b(x, h0, cbuf, w_in, conv_w, conv_b, wa, ba, wx, bx, lam, w_out, g, b, tb):
    B, T, _ = x.shape
    W = RNN_WIDTH
    nt = T // tb
    per_b = lambda r, c: pl.BlockSpec((1, r, c), lambda bi, ti: (bi, 0, 0))
    return pl.pallas_call(
        functools.partial(_rglru_body, tb=tb),
        grid=(B, nt),
        in_specs=[
            pl.BlockSpec((1, tb, D_MODEL), lambda bi, ti: (bi, ti, 0)),
            per_b(1, W), per_b(B_CONV - 1, W),
            _const_spec((D_MODEL, 2 * W)), _const_spec((B_CONV, W)), _const_spec((1, W)),
            _const_spec((W, W)), _const_spec((1, W)), _const_spec((W, W)), _const_spec((1, W)),
            _const_spec((1, W)), _const_spec((W, D_MODEL)), _const_spec((1, D_MODEL)), _const_spec((1, D_MODEL)),
        ],
        out_specs=[
            pl.BlockSpec((1, tb, D_MODEL), lambda bi, ti: (bi, ti, 0)),
            per_b(1, W), per_b(B_CONV - 1, W),
        ],
        out_shape=[jax.ShapeDtypeStruct((B, T, D_MODEL), f32),
                   jax.ShapeDtypeStruct((B, 1, W), f32),
                   jax.ShapeDtypeStruct((B, B_CONV - 1, W), f32)],
        scratch_shapes=[pltpu.VMEM((SUBLANES, W), f32), pltpu.VMEM((1, W), f32),
                        pltpu.VMEM((tb, W), f32), pltpu.VMEM((tb, W), f32), pltpu.VMEM((tb, W), f32)],
        compiler_params=_cparams(("arbitrary", "arbitrary")),
        name="rglru_mixer",
    )(x, h0, cbuf, w_in, conv_w, conv_b, wa, ba, wx, bx, lam, w_out, g, b)


C_HALO = 32


def _conformer_body(x_ref, cbuf_ref, win_ref, bin_ref, cw_ref, cb_ref, lg_ref, lb_ref, wout_ref, bout_ref,
                    g_ref, b_ref, o_ref, nbuf_ref, ext_ref, conv_ref, sh_ref, *, tb):
    t = pl.program_id(1)
    C = C_WIDTH
    hist = C_CONV - 1

    @pl.when(t == 0)
    def _():
        ext_ref[0:C_HALO - hist, :] = jnp.zeros((C_HALO - hist, C), f32)
        ext_ref[C_HALO - hist:C_HALO, :] = cbuf_ref[0]

    x = x_ref[0]
    z = jnp.dot(x.astype(bf16), win_ref[...], preferred_element_type=f32) + bin_ref[...]
    u = z[:, :C] * jax.nn.sigmoid(z[:, C:])
    ext_ref[C_HALO:C_HALO + tb, :] = u

    nsh = tb + C_HALO - SUBLANES
    for bres in range(1, SUBLANES):
        sh_ref[bres - 1] = ext_ref[pl.ds(bres, nsh), :]

    rt = min(tb, 128)
    for r0 in range(0, tb, rt):
        for c0 in range(0, C, LANES):
            acc = jnp.broadcast_to(cb_ref[:, c0:c0 + LANES], (rt, LANES))
            for k in range(C_CONV):
                a8, bres = divmod(k + C_HALO - hist, SUBLANES)
                if bres == 0:
                    win = ext_ref[pl.ds(r0 + SUBLANES * a8, rt), c0:c0 + LANES]
                else:
                    win = sh_ref[bres - 1, pl.ds(r0 + SUBLANES * a8, rt), c0:c0 + LANES]
                acc = acc + cw_ref[k:k + 1, c0:c0 + LANES] * win
            conv_ref[r0:r0 + rt, c0:c0 + LANES] = acc

    nbuf_ref[0] = ext_ref[tb + C_HALO - hist:tb + C_HALO, :]
    ext_ref[0:C_HALO, :] = ext_ref[tb:tb + C_HALO, :]

    c = _layer_norm(conv_ref[...], lg_ref[...], lb_ref[...])
    s = jax.nn.silu(c)
    out = jnp.dot(s.astype(bf16), wout_ref[...], preferred_element_type=f32) + bout_ref[...]
    o_ref[0] = _res_ln(x, out, g_ref[...], b_ref[...])


def _mixer_c(x, cbuf, w_in, b_in, conv_w, conv_b, ln_g, ln_b, w_out, b_out, g, b, tb):
    B, T, _ = x.shape
    C = C_WIDTH
    nt = T // tb
    return pl.pallas_call(
        functools.partial(_conformer_body, tb=tb),
        grid=(B, nt),
        in_specs=[
            pl.BlockSpec((1, tb, D_MODEL), lambda bi, ti: (bi, ti, 0)),
            pl.BlockSpec((1, C_CONV - 1, C), lambda bi, ti: (bi, 0, 0)),
            _const_spec((D_MODEL, 2 * C)), _const_spec((1, 2 * C)),
            _const_spec((C_CONV, C)), _const_spec((1, C)), _const_spec((1, C)), _const_spec((1, C)),
            _const_spec((C, D_MODEL)), _const_spec((1, D_MODEL)),
            _const_spec((1, D_MODEL)), _const_spec((1, D_MODEL)),
        ],
        out_specs=[
            pl.BlockSpec((1, tb, D_MODEL), lambda bi, ti: (bi, ti, 0)),
            pl.BlockSpec((1, C_CONV - 1, C), lambda bi, ti: (bi, 0, 0)),
        ],
        out_shape=[jax.ShapeDtypeStruct((B, T, D_MODEL), f32),
                   jax.ShapeDtypeStruct((B, C_CONV - 1, C), f32)],
        scratch_shapes=[pltpu.VMEM((tb + C_HALO, C), f32), pltpu.VMEM((tb, C), f32),
                        pltpu.VMEM((SUBLANES - 1, tb + C_HALO - SUBLANES, C), f32)],
        compiler_params=_cparams(("arbitrary", "arbitrary")),
        name="conformer_mixer",
    )(x, cbuf, w_in, b_in, conv_w, conv_b, ln_g, ln_b, w_out, b_out, g, b)


def _rwkv_front_body(x_ref, sh_ref, mu_ref, wr_ref, wk_ref, wv_ref, w0_ref, w1_ref, w2_ref,
                     a0_ref, a1_ref, a2_ref, g1_ref, g2_ref, kk_ref, ka_ref, mseg_ref,
                     r_o, lw_o, k_o, v_o, an_o, bk_o, g_o, shout_ref, car_ref, *, tb):
    t = pl.program_id(1)

    @pl.when(t == 0)
    def _():
        car_ref[...] = sh_ref[0]

    x = x_ref[0]
    (prev,) = _shift_rows(x, [car_ref[...]])
    car_ref[...] = x[tb - 1:tb, :]
    shout_ref[0] = x[tb - 1:tb, :]
    xx = prev - x
    xr, xw, xk, xv, xa, xg = (x + xx * mu_ref[n:n + 1, :] for n in range(6))
    r = _mm(xr, wr_ref[...])
    k = _mm(xk, wk_ref[...])
    v = _mm(xv, wv_ref[...])
    wl = w0_ref[...] + _mm(jnp.tanh(_mm(xw, w1_ref[...])), w2_ref[...])
    w_log = -jax.nn.softplus(-wl) - 0.5
    lw = -jnp.exp(w_log)
    a = jax.nn.sigmoid(a0_ref[...] + _mm(_mm(xa, a1_ref[...]), a2_ref[...]))
    g = _mm(jax.nn.sigmoid(_mm(xg, g1_ref[...])), g2_ref[...])
    kk = k * kk_ref[...]
    n2 = _dot2(kk * kk, mseg_ref[...])
    kk = kk / jnp.maximum(jnp.sqrt(n2), 1e-12)
    k2 = k * (1.0 + (a - 1.0) * ka_ref[...])
    outs = ((r_o, r), (lw_o, lw), (k_o, k2), (v_o, v), (an_o, -kk), (bk_o, kk * a), (g_o, g))
    for ref, val in outs:
        for p in range(D_PAIRS):
            ref[0, p] = val[:, p * LANES:(p + 1) * LANES]


def _rwkv_front(x, shift0, mu, w_r, w_k, w_v, w0, w1, w2, a0, a1, a2, g1, g2, k_k, k_a, mseg, tb):
    B, T, D = x.shape
    nt = T // tb
    pm = pl.BlockSpec((1, D_PAIRS, tb, LANES), lambda bi, ti: (bi, 0, ti, 0))
    pm_shape = jax.ShapeDtypeStruct((B, D_PAIRS, T, LANES), f32)
    vec = _const_spec((1, D))
    return pl.pallas_call(
        functools.partial(_rwkv_front_body, tb=tb),
        grid=(B, nt),
        in_specs=[
            pl.BlockSpec((1, tb, D), lambda bi, ti: (bi, ti, 0)),
            pl.BlockSpec((1, 1, D), lambda bi, ti: (bi, 0, 0)),
            _const_spec((6, D)), _const_spec((D, D)), _const_spec((D, D)), _const_spec((D, D)),
            vec, _const_spec((D, LORA_PAD)), _const_spec((LORA_PAD, D)),
            vec, _const_spec((D, LORA_PAD)), _const_spec((LORA_PAD, D)),
            _const_spec((D, LORA_PAD)), _const_spec((LORA_PAD, D)),
            vec, vec, _const_spec((D, D)),
        ],
        out_specs=[pm] * 7 + [pl.BlockSpec((1, 1, D), lambda bi, ti: (bi, 0, 0))],
        out_shape=[pm_shape] * 7 + [jax.ShapeDtypeStruct((B, 1, D), f32)],
        scratch_shapes=[pltpu.VMEM((1, D), f32)],
        compiler_params=_cparams(("arbitrary", "arbitrary")),
        name="rwkv_front",
    )(x, shift0, mu, w_r, w_k, w_v, w0, w1, w2, a0, a1, a2, g1, g2, k_k, k_a, mseg)


def _wkv_body(r_ref, lw_ref, k_ref, v_ref, an_ref, bk_ref, g_ref, s0_ref, rk_ref, gng_ref, gnb_ref,
              y_ref, sout_ref, s_ref, *, tb):
    t = pl.program_id(1)
    C = WKV_CHUNK
    P2 = 2 * C

    @pl.when(t == 0)
    def _():
        s_ref[...] = s0_ref[0]

    lane = lax.broadcasted_iota(i32, (1, LANES), 1)
    m0 = (lane < D_HEAD_DIM).astype(f32)
    m1 = 1.0 - m0
    ri = lax.broadcasted_iota(i32, (P2, P2), 0)
    ci = lax.broadcasted_iota(i32, (P2, P2), 1)
    same_head = (ri >= C) == (ci >= C)
    s_idx = ri & (C - 1)
    t_idx = ci & (C - 1)
    mask_strict = same_head & (s_idx < t_idx)
    mask_incl = same_head & (s_idx <= t_idx)
    eye = (ri == ci).astype(f32)
    seg_ones = ((ri >= D_HEAD_DIM) == (ci >= D_HEAD_DIM)).astype(f32)
    tr = lax.broadcasted_iota(i32, (C, C), 0)
    tc = lax.broadcasted_iota(i32, (C, C), 1)
    tri_incl = (tc <= tr).astype(f32)

    def stack2(z):
        return jnp.concatenate([z * m0, z * m1], axis=0)

    dims_nt = ((1,), (1,))
    dims_tn = ((0,), (0,))
    inv = 1.0 / D_HEAD_DIM
    PR = range(D_PAIRS)
    units = [(c, p) for c in range(tb // C) for p in PR]
    UN = range(len(units))
    sls = [slice(c * C, (c + 1) * C) for c, _ in units]
    prs = [p for _, p in units]
    r = [r_ref[0, prs[u], sls[u], :] for u in UN]
    lw = [lw_ref[0, prs[u], sls[u], :] for u in UN]
    k = [k_ref[0, prs[u], sls[u], :] for u in UN]
    v = [v_ref[0, prs[u], sls[u], :] for u in UN]
    an = [an_ref[0, prs[u], sls[u], :] for u in UN]
    bk = [bk_ref[0, prs[u], sls[u], :] for u in UN]
    cs = [_dot2l(tri_incl, lw[u]) for u in UN]
    cs_end = [cs[u][C - 1:C, :] for u in UN]
    ah = [an[u] * jnp.exp(cs[u] - lw[u]) for u in UN]
    p_inv = [jnp.exp(-cs[u]) for u in UN]
    bh = [bk[u] * p_inv[u] for u in UN]
    kh = [k[u] * p_inv[u] for u in UN]
    rh = [r[u] * jnp.exp(cs[u]) for u in UN]
    p_tail = [jnp.exp(cs_end[u] - cs[u]) for u in UN]
    X = [jnp.concatenate([stack2(bh[u]), stack2(kh[u])], axis=0) for u in UN]
    Ga = [_mm_nt(X[u], jnp.concatenate([ah[u], ah[u]], axis=0)) for u in UN]
    Gr = [_mm_nt(X[u], jnp.concatenate([rh[u], rh[u]], axis=0)) for u in UN]
    n_ab = [jnp.where(mask_strict, Ga[u][0:P2], 0.0) for u in UN]
    n_ak = [jnp.where(mask_strict, Ga[u][P2:2 * P2], 0.0) for u in UN]
    n_rb = [jnp.where(mask_incl, Gr[u][0:P2], 0.0).astype(bf16) for u in UN]
    n_rk = [jnp.where(mask_incl, Gr[u][P2:2 * P2], 0.0).astype(bf16) for u in UN]
    tt = [eye + n_ab[u] for u in UN]
    mpow = n_ab
    steps = 1
    while steps * 2 < C:
        mpow = [_mm(mpow[u], mpow[u]) for u in UN]
        tt = [tt[u] + _mm(tt[u], mpow[u]) for u in UN]
        steps *= 2
    a2 = [stack2(ah[u]) for u in UN]
    r2 = [stack2(rh[u]) for u in UN]
    v2 = [stack2(v[u]) for u in UN]
    for c in range(tb // C):
        CU = [u for u in UN if units[u][0] == c]
        S = {u: s_ref[prs[u]] for u in CU}
        rhs = {u: _dot3(a2[u], S[u], dims_nt) + _dot3(n_ak[u], v2[u], dims_tn) for u in CU}
        u2 = {u: _dot3(tt[u], rhs[u], dims_tn) for u in CU}
        for u in CU:
            bt2 = stack2(bk[u] * p_tail[u])
            kt2 = stack2(k[u] * p_tail[u])
            s_ref[prs[u]] = (S[u] * jnp.exp(cs_end[u]) + _dot3(u2[u], bt2, dims_tn)
                             + _dot3(v2[u], kt2, dims_tn))
        y2 = {u: _mm_nt(r2[u], S[u]) + _dg(n_rb[u], u2[u].astype(bf16), dims_tn)
              + _dg(n_rk[u], v2[u].astype(bf16), dims_tn) for u in CU}
        for u in CU:
            p = prs[u]
            y = y2[u][0:C] + y2[u][C:P2]
            mu_y = _dot2(y, seg_ones) * inv
            d = y - mu_y
            var = _dot2(d * d, seg_ones) * inv
            yn = d * lax.rsqrt(var + GN_EPS) * gng_ref[p] + gnb_ref[p]
            bonus = _dot2(r[u] * k[u] * rk_ref[p], seg_ones) * v[u]
            y_ref[0, p, sls[u], :] = (yn + bonus) * g_ref[0, p, sls[u], :]

    sout_ref[0] = s_ref[...]


def _wkv(r, lw, k, v, an, bk, g, s0, r_k, gn_g, gn_b, tb):
    B, NP, T, _ = r.shape
    nt = T // tb
    pm = pl.BlockSpec((1, NP, tb, LANES), lambda bi, ti: (bi, 0, ti, 0))
    st = pl.BlockSpec((1, NP, LANES, LANES), lambda bi, ti: (bi, 0, 0, 0))
    par = _const_spec((NP, 1, LANES))
    return pl.pallas_call(
        functools.partial(_wkv_body, tb=tb),
        grid=(B, nt),
        in_specs=[pm] * 7 + [st, par, par, par],
        out_specs=[pm, st],
        out_shape=[jax.ShapeDtypeStruct((B, NP, T, LANES), f32),
                   jax.ShapeDtypeStruct((B, NP, LANES, LANES), f32)],
        scratch_shapes=[pltpu.VMEM((NP, LANES, LANES), f32)],
        compiler_params=_cparams(("arbitrary", "arbitrary")),
        name="rwkv_wkv",
    )(r, lw, k, v, an, bk, g, s0, r_k, gn_g, gn_b)


def _rwkv_out_body(y_ref, x_ref, wo_ref, g_ref, b_ref, o_ref):
    y = jnp.concatenate([y_ref[0, p] for p in range(D_PAIRS)], axis=1)
    out = jnp.dot(y.astype(bf16), wo_ref[...], preferred_element_type=f32)
    o_ref[0] = _res_ln(x_ref[0], out, g_ref[...], b_ref[...])


def _rwkv_out(y_pm, x, w_o, g, b, tb):
    B, T, D = x.shape
    nt = T // tb
    return pl.pallas_call(
        _rwkv_out_body,
        grid=(B, nt),
        in_specs=[pl.BlockSpec((1, D_PAIRS, tb, LANES), lambda bi, ti: (bi, 0, ti, 0)),
                  pl.BlockSpec((1, tb, D), lambda bi, ti: (bi, ti, 0)),
                  _const_spec((D, D)), _const_spec((1, D)), _const_spec((1, D))],
        out_specs=pl.BlockSpec((1, tb, D), lambda bi, ti: (bi, ti, 0)),
        out_shape=jax.ShapeDtypeStruct((B, T, D), f32),
        compiler_params=_cparams(("arbitrary", "arbitrary")),
        name="rwkv_out",
    )(y_pm, x, w_o, g, b)


def _rope_tables(pos, head_dim, reps):
    half = head_dim // 2
    inv_freq = jnp.exp(-math.log(ROPE_THETA) * jnp.arange(half, dtype=f32) / half)
    ang = pos.astype(f32)[:, None] * inv_freq[None, :]
    cos, sin = jnp.cos(ang), jnp.sin(ang)
    cos_t = jnp.tile(jnp.concatenate([cos, cos], axis=1), (1, reps))
    sin_t = jnp.tile(jnp.concatenate([-sin, sin], axis=1), (1, reps))
    return cos_t, sin_t


def _pad_a_w_in(w):
    q = w[:, :A_QW]
    k = w[:, A_QW:A_QW + A_KW]
    v = w[:, A_QW + A_KW:A_QW + 2 * A_KW]
    o = A_QW + 2 * A_KW
    qi = w[:, o:o + IDX_HEADS * IDX_DIM].reshape(D_MODEL, IDX_HEADS, IDX_DIM)
    qi = jnp.pad(qi, ((0, 0), (0, 0), (0, LANES - IDX_DIM))).reshape(D_MODEL, IDX_HEADS * LANES)
    o += IDX_HEADS * IDX_DIM
    ki = jnp.pad(w[:, o:o + IDX_DIM], ((0, 0), (0, LANES - IDX_DIM)))
    o += IDX_DIM
    wi = jnp.pad(w[:, o:o + IDX_HEADS], ((0, 0), (0, LANES - IDX_HEADS)))
    return jnp.concatenate([q, k, v, qi, ki, wi], axis=1).astype(bf16)


def _block_diag(w):
    n, d, _ = w.shape
    eye = jnp.eye(n, dtype=w.dtype)
    return (eye[:, None, :, None] * w[:, :, None, :]).reshape(n * d, n * d)


def _pair_state(s):
    B = s.shape[0]
    s = s.reshape(B, D_PAIRS, 2, D_HEAD_DIM, D_HEAD_DIM)
    z = jnp.zeros_like(s[:, :, 0])
    top = jnp.concatenate([s[:, :, 0], z], axis=-1)
    bot = jnp.concatenate([z, s[:, :, 1]], axis=-1)
    return jnp.concatenate([top, bot], axis=-2)


def _unpair_state(sp):
    B = sp.shape[0]
    h0 = sp[:, :, :D_HEAD_DIM, :D_HEAD_DIM]
    h1 = sp[:, :, D_HEAD_DIM:, D_HEAD_DIM:]
    return jnp.stack([h0, h1], axis=2).reshape(B, D_HEADS, D_HEAD_DIM, D_HEAD_DIM)


def _row(v):
    return v.reshape(1, -1).astype(f32)


def _mixer_a_stream(x, w_pad, w_out, g, b, pos, cache, *, tb, tq, kb_size):
    B, T, D = x.shape
    R = B * T
    c128, s128 = _rope_tables(pos, A_HEAD_DIM, 1)
    c64, s64 = _rope_tables(pos, IDX_DIM, 2)
    if tb > T:
        rep = tb // T
        c128, s128, c64, s64 = (jnp.tile(a, (rep, 1)) for a in (c128, s128, c64, s64))
    x2d = x.reshape(R, D)
    q_hm, k_o, v_o, kb, vb, qi_hm, ki_o, kib, wi = _a_project(x2d, w_pad, c128, s128, c64, s64, tb)
    kb3, vb3, kib3 = kb.reshape(B, T, A_KW), vb.reshape(B, T, A_KW), kib.reshape(B, T, LANES)
    if cache is None:
        q_pos0, l_real = 0, T
    else:
        ck, cv, cki = cache
        P = ck.shape[1]
        kb3 = jnp.concatenate([ck.reshape(B, P, A_KW).astype(bf16), kb3], axis=1)
        vb3 = jnp.concatenate([cv.reshape(B, P, A_KW).astype(bf16), vb3], axis=1)
        cki_p = jnp.pad(cki, ((0, 0), (0, 0), (0, LANES - IDX_DIM))).astype(bf16)
        kib3 = jnp.concatenate([cki_p, kib3], axis=1)
        q_pos0, l_real = P, P + T
    l_pad = -(-l_real // kb_size) * kb_size
    if l_pad > l_real:
        padk = ((0, 0), (0, l_pad - l_real), (0, 0))
        kb3, vb3, kib3 = jnp.pad(kb3, padk), jnp.pad(vb3, padk), jnp.pad(kib3, padk)
    topk = min(TOPK_MAX, l_real // 4)
    xn = _dsa_attend(q_hm, qi_hm, wi, x2d, kb3, vb3, kib3, w_out, g, b,
                     n_batch=B, tq=tq, kb_size=kb_size, q_pos0=q_pos0, l_real=l_real, topk=topk)
    return (xn.reshape(B, T, D), k_o.reshape(B, T, A_KV_HEADS, A_HEAD_DIM),
            v_o.reshape(B, T, A_KV_HEADS, A_HEAD_DIM), ki_o.reshape(B, T, IDX_DIM))


def _mixer_d_stream(x, wkv0, shift0, prm, g, b, *, tb, tb_wkv):
    B, T, D = x.shape
    outs = _rwkv_front(x, shift0.reshape(B, 1, D), prm["mu"], prm["w_r"], prm["w_k"], prm["w_v"], prm["w0"],
                       prm["w1"], prm["w2"], prm["a0"], prm["a1"], prm["a2"], prm["g1"], prm["g2"],
                       prm["k_k"], prm["k_a"], prm["mseg"], tb)
    seq, shift_out = outs[:7], outs[7]
    t_pad = -(-T // tb_wkv) * tb_wkv
    if t_pad > T:
        seq = [jnp.pad(a, ((0, 0), (0, 0), (0, t_pad - T), (0, 0))) for a in seq]
    y_pm, s_out = _wkv(*seq, _pair_state(wkv0), prm["r_k"], prm["gn_g"], prm["gn_b"], tb_wkv)
    if t_pad > T:
        y_pm = y_pm[:, :, :T]
    xn = _rwkv_out(y_pm, x, prm["w_o"], g, b, tb)
    return xn, _unpair_state(s_out), shift_out.reshape(B, D)


def kernel(x_prompt, x_sample, cache_a_k, cache_a_v, cache_a_kidx, state_b_h, state_b_conv, state_c_conv,
           state_d_wkv, state_d_shift, state_f_conv, ln_g, ln_b, a_w_in, a_w_out,
           b_w_in, b_conv_w, b_conv_b, b_gate_a_w, b_gate_a_b, b_gate_x_w, b_gate_x_b, b_lambda, b_w_out,
           c_w_in, c_b_in, c_conv_w, c_conv_b, c_ln_g, c_ln_b, c_w_out, c_b_out,
           d_mu, d_w_r, d_w_k, d_w_v, d_w_o, d_w0, d_w1, d_w2, d_a0, d_a1, d_a2, d_g1, d_g2,
           d_k_k, d_k_a, d_r_k, d_gn_g, d_gn_b, f_w_up, f_conv_w, f_conv_b, f_w_down):
    xp, xs = x_prompt, x_sample
    BP, TP, D = xp.shape
    BS, TS, _ = xs.shape
    tb_p = 256
    tb_s = TS
    outs = {k: [] for k in ("ak_p", "ak_s", "av_p", "av_s", "aki_p", "aki_s", "bh_p", "bh_s", "bc_p", "bc_s",
                            "cc_p", "cc_s", "dw_p", "dw_s", "dsh_p", "dsh_s", "fc_p", "fc_s")}
    for i in range(DEPTH):
        m, j = i % 4, i // 4
        g0, b0 = _row(ln_g[i, 0]), _row(ln_b[i, 0])
        g1, b1 = _row(ln_g[i, 1]), _row(ln_b[i, 1])
        if m == 0:
            w_pad = _pad_a_w_in(a_w_in[j])
            w_out = a_w_out[j].astype(bf16)
            xp, kp, vp, kip = _mixer_a_stream(xp, w_pad, w_out, g0, b0, jnp.arange(TP, dtype=i32), None,
                                              tb=512, tq=128, kb_size=1024)
            P = cache_a_k.shape[2]
            xs, ks, vs, kis = _mixer_a_stream(xs, w_pad, w_out, g0, b0, P + jnp.arange(TS, dtype=i32),
                                              (cache_a_k[j], cache_a_v[j], cache_a_kidx[j]),
                                              tb=BS * TS, tq=TS, kb_size=256)
            outs["ak_p"].append(kp); outs["ak_s"].append(ks)
            outs["av_p"].append(vp); outs["av_s"].append(vs)
            outs["aki_p"].append(kip); outs["aki_s"].append(kis)
        elif m == 1:
            W = RNN_WIDTH
            args = (b_w_in[j].astype(bf16), b_conv_w[j], _row(b_conv_b[j]),
                    _block_diag(b_gate_a_w[j]).astype(bf16), _row(b_gate_a_b[j]),
                    _block_diag(b_gate_x_w[j]).astype(bf16), _row(b_gate_x_b[j]),
                    _row(b_lambda[j]), b_w_out[j].astype(bf16), g0, b0)
            xp, hp, cp = _mixer_b(xp, jnp.zeros((BP, 1, W), f32), jnp.zeros((BP, B_CONV - 1, W), f32), *args, tb_p)
            xs, hs, cs = _mixer_b(xs, state_b_h[j].reshape(BS, 1, W), state_b_conv[j], *args, tb_s)
            outs["bh_p"].append(hp.reshape(BP, W)); outs["bh_s"].append(hs.reshape(BS, W))
            outs["bc_p"].append(cp); outs["bc_s"].append(cs)
        elif m == 2:
            args = (c_w_in[j].astype(bf16), _row(c_b_in[j]), c_conv_w[j], _row(c_conv_b[j]),
                    _row(c_ln_g[j]), _row(c_ln_b[j]), c_w_out[j].astype(bf16), _row(c_b_out[j]), g0, b0)
            xp, cp = _mixer_c(xp, jnp.zeros((BP, C_CONV - 1, C_WIDTH), f32), *args, tb_p)
            xs, cs = _mixer_c(xs, state_c_conv[j], *args, tb_s)
            outs["cc_p"].append(cp); outs["cc_s"].append(cs)
        else:
            lp = ((0, 0), (0, LORA_PAD - d_w1.shape[2]))
            lq = ((0, LORA_PAD - d_w1.shape[2]), (0, 0))
            gp = ((0, 0), (0, LORA_PAD - d_g1.shape[2]))
            gq = ((0, LORA_PAD - d_g1.shape[2]), (0, 0))
            pairs = lambda v: v.reshape(D_PAIRS, 1, LANES).astype(f32)
            prm = dict(
                mu=d_mu[j], w_r=d_w_r[j].astype(bf16), w_k=d_w_k[j].astype(bf16), w_v=d_w_v[j].astype(bf16),
                w_o=d_w_o[j].astype(bf16), w0=_row(d_w0[j]),
                w1=jnp.pad(d_w1[j], lp).astype(bf16), w2=jnp.pad(d_w2[j], lq).astype(bf16),
                a0=_row(d_a0[j]), a1=jnp.pad(d_a1[j], lp).astype(bf16), a2=jnp.pad(d_a2[j], lq).astype(bf16),
                g1=jnp.pad(d_g1[j], gp).astype(bf16), g2=jnp.pad(d_g2[j], gq).astype(bf16),
                k_k=_row(d_k_k[j]), k_a=_row(d_k_a[j]),
                mseg=_block_diag(jnp.ones((D_HEADS, D_HEAD_DIM, D_HEAD_DIM), bf16)),
                r_k=pairs(d_r_k[j]), gn_g=pairs(d_gn_g[j]), gn_b=pairs(d_gn_b[j]))
            xp, sp, shp = _mixer_d_stream(xp, jnp.zeros((BP, D_HEADS, D_HEAD_DIM, D_HEAD_DIM), f32),
                                          jnp.zeros((BP, D), f32), prm, g0, b0, tb=tb_p, tb_wkv=2 * WKV_CHUNK)
            xs, ss, shs = _mixer_d_stream(xs, state_d_wkv[j], state_d_shift[j], prm, g0, b0,
                                          tb=tb_s, tb_wkv=WKV_CHUNK)
            outs["dw_p"].append(sp); outs["dw_s"].append(ss)
            outs["dsh_p"].append(shp); outs["dsh_s"].append(shs)
        fargs = (f_w_up[i].astype(bf16), f_conv_w[i], _row(f_conv_b[i]), f_w_down[i].astype(bf16), g1, b1)
        xp, fbp = _conv_ffn(xp, jnp.zeros((BP, F_CONV - 1, 2 * D_FF), f32), *fargs, tb_p)
        xs, fbs = _conv_ffn(xs, state_f_conv[i], *fargs, tb_s)
        outs["fc_p"].append(fbp); outs["fc_s"].append(fbs)
    st = lambda k: jnp.stack(outs[k])
    return (xp, xs,
            st("ak_p"), st("ak_s"), st("av_p"), st("av_s"), st("aki_p"), st("aki_s"),
            st("bh_p"), st("bh_s"), st("bc_p"), st("bc_s"), st("cc_p"), st("cc_s"),
            st("dw_p"), st("dw_s"), st("dsh_p"), st("dsh_s"), st("fc_p"), st("fc_s"))
```

```python
import functools
import math

import jax
import jax.numpy as jnp
from jax import lax
from jax.experimental import pallas as pl
from jax.experimental.pallas import tpu as pltpu

f32 = jnp.float32
bf16 = jnp.bfloat16
i32 = jnp.int32

D_MODEL = 1024
DEPTH = 4
CHUNK = 64
ALPHA = (2.0 * DEPTH) ** 0.25
LN_EPS = 1e-5
NEG_INF = -1e30

A_HEADS = 8
A_KV_HEADS = 2
A_HEAD_DIM = 128
A_GROUP = A_HEADS // A_KV_HEADS
IDX_HEADS = 8
IDX_DIM = 64
IDX_SCALE = (IDX_DIM ** -0.5) * (IDX_HEADS ** -0.5)
TOPK_MAX = 256
ROPE_THETA = 10000.0

RNN_WIDTH = 1408
LRU_BLOCKS = 8
LRU_BLOCK_DIM = RNN_WIDTH // LRU_BLOCKS
B_CONV = 4
LRU_C = 8.0

C_WIDTH = 1024
C_CONV = 31

D_HEAD_DIM = 64
D_HEADS = D_MODEL // D_HEAD_DIM
D_PAIRS = D_HEADS // 2
GN_EPS = 64e-5
LORA_PAD = 128

D_FF = 2816
F_CONV = 3

LANES = 128
SUBLANES = 8
VMEM_LIMIT_BYTES = 60 * 1024 * 1024
WKV_CHUNK = 64

INT_MIN = -(2 ** 31)
INT_MAX = 2 ** 31 - 1


def _cparams(sem):
    return pltpu.CompilerParams(dimension_semantics=sem, vmem_limit_bytes=VMEM_LIMIT_BYTES)


def _const_spec(shape):
    nd = len(shape)
    return pl.BlockSpec(shape, lambda *_: (0,) * nd, pipeline_mode=pl.Buffered(1))


def _mm(a, b):
    return jnp.dot(a.astype(bf16), b.astype(bf16), preferred_element_type=f32)


def _split(a):
    hi = a.astype(bf16)
    lo = (a - hi.astype(f32)).astype(bf16)
    return hi, lo


def _dg(a, b, dims):
    return lax.dot_general(a, b, (dims, ((), ())), preferred_element_type=f32)


def _mm_nt(a, b):
    return _dg(a.astype(bf16), b.astype(bf16), ((1,), (1,)))


def _dot3(a, b, dims=((1,), (0,))):
    ah, al = _split(a)
    bh, bl = _split(b)
    return _dg(ah, bh, dims) + (_dg(ah, bl, dims) + _dg(al, bh, dims))


def _dot2(a, b_exact, dims=((1,), (0,))):
    ah, al = _split(a)
    bb = b_exact.astype(bf16)
    return _dg(ah, bb, dims) + _dg(al, bb, dims)


def _dot2l(a_exact, b, dims=((1,), (0,))):
    bh, bl = _split(b)
    aa = a_exact.astype(bf16)
    return _dg(aa, bh, dims) + _dg(aa, bl, dims)


def _layer_norm(z, g, b, eps=LN_EPS):
    mu = jnp.mean(z, axis=-1, keepdims=True)
    zc = z - mu
    var = jnp.mean(zc * zc, axis=-1, keepdims=True)
    return zc * lax.rsqrt(var + eps) * g + b


def _res_ln(x, y, g, b):
    return _layer_norm(ALPHA * x + y, g, b)


def _shift_rows(h, carry_rows):
    n = len(carry_rows)
    row = lax.broadcasted_iota(i32, (SUBLANES, h.shape[1]), 0)
    out = []
    for k in range(1, n + 1):
        hk = pltpu.roll(h, k, axis=0)
        top = hk[0:SUBLANES]
        for j in range(k):
            top = jnp.where(row == j, carry_rows[n - (k - j)], top)
        out.append(jnp.concatenate([top, hk[SUBLANES:]], axis=0) if h.shape[0] > SUBLANES else top)
    return out


FFN_TN = 256
FFN_LOOKAHEAD = 2


def _ffn_body(x_ref, buf_ref, wup_ref, cw_ref, cb_ref, wdn_ref, g_ref, b_ref,
              o_ref, nbuf_ref, carry_ref, *, tb):
    t = pl.program_id(1)

    @pl.when(t == 0)
    def _():
        carry_ref[6:8, :] = buf_ref[0]

    x = x_ref[0]
    xb = x.astype(bf16)
    nj = D_FF // FFN_TN

    def up(j):
        return [jnp.dot(xb, wup_ref[:, part * D_FF + j * FFN_TN: part * D_FF + (j + 1) * FFN_TN],
                        preferred_element_type=f32) for part in range(2)]

    acc = jnp.zeros((tb, D_MODEL), f32)
    ahead = [up(jj) for jj in range(min(FFN_LOOKAHEAD, nj))]
    for j in range(nj):
        hs = ahead.pop(0)
        if j + FFN_LOOKAHEAD < nj:
            ahead.append(up(j + FFN_LOOKAHEAD))
        us = []
        for part in range(2):
            c0 = part * D_FF + j * FFN_TN
            h = hs[part]
            cm2 = carry_ref[6:7, c0:c0 + FFN_TN]
            cm1 = carry_ref[7:8, c0:c0 + FFN_TN]
            h1, h2 = _shift_rows(h, [cm2, cm1])
            u = (cw_ref[0:1, c0:c0 + FFN_TN] * h2 + cw_ref[1:2, c0:c0 + FFN_TN] * h1
                 + cw_ref[2:3, c0:c0 + FFN_TN] * h + cb_ref[:, c0:c0 + FFN_TN])
            carry_ref[6:8, c0:c0 + FFN_TN] = h[tb - 2:tb, :]
            us.append(u)
        act = jax.nn.silu(us[0]) * us[1]
        acc = acc + jnp.dot(act.astype(bf16), wdn_ref[j * FFN_TN:(j + 1) * FFN_TN, :],
                            preferred_element_type=f32)
    nbuf_ref[0] = carry_ref[6:8, :]
    o_ref[0] = _res_ln(x, acc, g_ref[...], b_ref[...])


def _conv_ffn(x, buf, w_up, conv_w, conv_b, w_down, g, b, tb):
    B, T, _ = x.shape
    nt = T // tb
    return pl.pallas_call(
        functools.partial(_ffn_body, tb=tb),
        grid=(B, nt),
        in_specs=[
            pl.BlockSpec((1, tb, D_MODEL), lambda bi, ti: (bi, ti, 0)),
            pl.BlockSpec((1, F_CONV - 1, 2 * D_FF), lambda bi, ti: (bi, 0, 0)),
            _const_spec((D_MODEL, 2 * D_FF)),
            _const_spec((F_CONV, 2 * D_FF)),
            _const_spec((1, 2 * D_FF)),
            _const_spec((D_FF, D_MODEL)),
            _const_spec((1, D_MODEL)),
            _const_spec((1, D_MODEL)),
        ],
        out_specs=[
            pl.BlockSpec((1, tb, D_MODEL), lambda bi, ti: (bi, ti, 0)),
            pl.BlockSpec((1, F_CONV - 1, 2 * D_FF), lambda bi, ti: (bi, 0, 0)),
        ],
        out_shape=[jax.ShapeDtypeStruct((B, T, D_MODEL), f32),
                   jax.ShapeDtypeStruct((B, F_CONV - 1, 2 * D_FF), f32)],
        scratch_shapes=[pltpu.VMEM((SUBLANES, 2 * D_FF), f32)],
        compiler_params=_cparams(("arbitrary", "arbitrary")),
        name="conv_ffn",
    )(x, buf, w_up, conv_w, conv_b, w_down, g, b)


A_QW = A_HEADS * A_HEAD_DIM
A_KW = A_KV_HEADS * A_HEAD_DIM
A_PROJ_PAD = A_QW + 2 * A_KW + IDX_HEADS * LANES + LANES + LANES
LOG2_E = math.log2(math.e)
TOPM = 12
TOPM_ROWS = 16
ATTN_ROW_CHUNK = 32


def _rope128(z, cos, sin):
    return z * cos + pltpu.roll(z, A_HEAD_DIM // 2, axis=1) * sin


def _rope64(z, cos, sin):
    lane = lax.broadcasted_iota(i32, z.shape, 1)
    rot = jnp.where((lane & 32) == 0, pltpu.roll(z, LANES - 32, axis=1), pltpu.roll(z, 32, axis=1))
    return z * cos + rot * sin


def _aproj_body(x_ref, w_ref, c128_ref, s128_ref, c64_ref, s64_ref,
                q_ref, k_ref, v_ref, kb_ref, vb_ref, qi_ref, ki_ref, kib_ref, wi_ref):
    xb = x_ref[...].astype(bf16)
    c128, s128 = c128_ref[...], s128_ref[...]
    c64, s64 = c64_ref[...], s64_ref[...]
    qscale = A_HEAD_DIM ** -0.5 * LOG2_E

    def proj2(c0):
        z = jnp.dot(xb, w_ref[:, c0:c0 + 2 * LANES], preferred_element_type=f32)
        return z[:, :LANES], z[:, LANES:]

    for h in range(0, A_HEADS, 2):
        for hh, z in zip((h, h + 1), proj2(h * LANES)):
            q_ref[hh] = (_rope128(z, c128, s128) * qscale).astype(bf16)
    for h, z in enumerate(proj2(A_QW)):
        kr = _rope128(z, c128, s128)
        k_ref[:, h * LANES:(h + 1) * LANES] = kr
        kb_ref[:, h * LANES:(h + 1) * LANES] = kr.astype(bf16)
    for h, vv in enumerate(proj2(A_QW + A_KW)):
        v_ref[:, h * LANES:(h + 1) * LANES] = vv
        vb_ref[:, h * LANES:(h + 1) * LANES] = vv.astype(bf16)
    base = A_QW + 2 * A_KW
    for h in range(0, IDX_HEADS, 2):
        for hh, z in zip((h, h + 1), proj2(base + h * LANES)):
            qi_ref[hh] = _rope64(z, c64, s64).astype(bf16)
    base = base + IDX_HEADS * LANES
    zk, zw = proj2(base)
    kir = _rope64(zk, c64, s64)
    ki_ref[...] = kir[:, :IDX_DIM]
    kib_ref[...] = kir.astype(bf16)
    wi_ref[...] = zw * IDX_SCALE


def _a_project(x2d, w_pad, c128, s128, c64, s64, tb):
    R = x2d.shape[0]
    nblk = R // tb
    ntab = c128.shape[0] // tb
    tab = pl.BlockSpec((tb, LANES), lambda i: (i % ntab, 0))
    row = lambda w: pl.BlockSpec((tb, w), lambda i: (i, 0))
    hm = pl.BlockSpec((A_HEADS, tb, LANES), lambda i: (0, i, 0))
    return pl.pallas_call(
        _aproj_body,
        grid=(nblk,),
        in_specs=[row(D_MODEL), _const_spec((D_MODEL, A_PROJ_PAD)), tab, tab, tab, tab],
        out_specs=[hm, row(A_KW), row(A_KW), row(A_KW), row(A_KW), hm, row(IDX_DIM), row(LANES), row(LANES)],
        out_shape=[
            jax.ShapeDtypeStruct((A_HEADS, R, LANES), bf16),
            jax.ShapeDtypeStruct((R, A_KW), f32),
            jax.ShapeDtypeStruct((R, A_KW), f32),
            jax.ShapeDtypeStruct((R, A_KW), bf16),
            jax.ShapeDtypeStruct((R, A_KW), bf16),
            jax.ShapeDtypeStruct((IDX_HEADS, R, LANES), bf16),
            jax.ShapeDtypeStruct((R, IDX_DIM), f32),
            jax.ShapeDtypeStruct((R, LANES), bf16),
            jax.ShapeDtypeStruct((R, LANES), f32),
        ],
        compiler_params=_cparams(("arbitrary",)),
        name="dsa_project",
    )(x2d, w_pad, c128, s128, c64, s64)


def _mono_key(s):
    i = lax.bitcast_convert_type(s, i32)
    return i ^ ((i >> 31) & jnp.int32(INT_MAX))


def _key_to_f32(k):
    return lax.bitcast_convert_type(k ^ ((k >> 31) & jnp.int32(INT_MAX)), f32)


KEY_PAD = -(2 ** 31) + 0x7FFFFF


def _dsa_body(q_ref, qi_ref, wi_ref, x_ref, k_hbm, v_hbm, ki_hbm, wo_ref, g_ref, b_ref,
              o_ref, k_vm, v_vm, ki_vm, keys_ref, cand_ref, candt_ref, m_ref, l_ref, alpha_ref, acc_ref, s_ref, p_ref,
              bias_ref, sem,
              *, tq, kb_size, nkb_total, q_pos0, l_real, topk):
    bi = pl.program_id(0)
    qi_blk = pl.program_id(1)

    @pl.when(qi_blk == 0)
    def _():
        copies = [pltpu.make_async_copy(k_hbm.at[bi], k_vm, sem.at[0]),
                  pltpu.make_async_copy(v_hbm.at[bi], v_vm, sem.at[1]),
                  pltpu.make_async_copy(ki_hbm.at[bi], ki_vm, sem.at[2])]
        for c in copies:
            c.start()
        for c in copies:
            c.wait()

    q0 = q_pos0 + qi_blk * tq
    last_chunk_end = ((q0 + tq - 1) // CHUNK + 1) * CHUNK
    nkb = jnp.minimum((last_chunk_end + kb_size - 1) // kb_size, nkb_total)

    qpos = q0 + lax.broadcasted_iota(i32, (tq, 1), 0)
    qchunk = qpos >> 6
    lane_k = lax.broadcasted_iota(i32, (1, kb_size), 1)

    def valid_mask(kb):
        kpos = kb * kb_size + lane_k
        ok = (kpos >> 6) <= qchunk
        return ok, kpos

    qi_all = qi_ref[...].reshape(IDX_HEADS * tq, LANES)
    wi = wi_ref[...]
    wcols = [wi[:, h:h + 1] for h in range(IDX_HEADS)]

    def score_block(kb):
        off = pl.multiple_of(kb * kb_size, kb_size)
        kiblk = ki_vm[pl.ds(off, kb_size), :]
        rel = _dg(qi_all, kiblk, ((1,), (1,))).reshape(IDX_HEADS, tq, kb_size)
        score = jnp.maximum(rel[0], 0.0) * wcols[0]
        for h in range(1, IDX_HEADS):
            score = score + jnp.maximum(rel[h], 0.0) * wcols[h]
        adm, kpos = valid_mask(kb)
        score = jnp.where(score == 0.0, 0.0, score)
        key = _mono_key(jnp.where(adm, score, NEG_INF))
        if l_real < nkb_total * kb_size:
            key = jnp.where(kpos < l_real, key, jnp.int32(KEY_PAD))
        keys_ref[kb] = key

    def score_pair(i, carry):
        score_block(2 * i)
        score_block(jnp.minimum(2 * i + 1, nkb - 1))
        return carry

    lax.fori_loop(0, (nkb + 1) // 2, score_pair, 0)

    nchunk = kb_size // LANES

    def count_where(pred):
        def body(kb, accv):
            for c in range(nchunk):
                blk = keys_ref[kb, :, c * LANES:(c + 1) * LANES]
                accv = accv + pred(blk, kb * kb_size + c * LANES).astype(i32)
            return accv
        accv = lax.fori_loop(0, nkb, body, jnp.zeros((tq, LANES), i32))
        return jnp.sum(accv, axis=1, keepdims=True)

    def bisect(count_ge):
        def bit_step(it, carry):
            p, cnt_p = carry
            bit = 31 - it
            cand = p | (jnp.int32(1) << bit)
            cnt = count_ge(jnp.broadcast_to(cand ^ jnp.int32(INT_MIN), (tq, LANES)))
            ok = cnt >= topk
            return jnp.where(ok, cand, p), jnp.where(ok, cnt, cnt_p)
        return lax.fori_loop(0, 32, bit_step, (jnp.zeros((tq, 1), i32), jnp.zeros((tq, 1), i32)))

    def lane_topm(rg, carry):
        r0 = pl.multiple_of(rg * TOPM_ROWS, TOPM_ROWS)

        def body(kb, lists):
            for c in range(nchunk):
                x = _key_to_f32(keys_ref[kb, pl.ds(r0, TOPM_ROWS), c * LANES:(c + 1) * LANES])
                new = []
                for li in lists:
                    new.append(jnp.maximum(li, x))
                    x = jnp.minimum(li, x)
                lists = tuple(new)
            return lists

        init = tuple(jnp.full((TOPM_ROWS, LANES), -jnp.inf, f32) for _ in range(TOPM))
        lists = lax.fori_loop(0, nkb, body, init)
        for i in range(TOPM):
            cand_ref[i, pl.ds(r0, TOPM_ROWS), :] = _mono_key(lists[i])
        return carry

    lax.fori_loop(0, tq // TOPM_ROWS, lane_topm, 0)

    def count_cand_ge(thr_b):
        accv = (cand_ref[0] >= thr_b).astype(i32)
        for i in range(1, TOPM):
            accv = accv + (cand_ref[i] >= thr_b).astype(i32)
        return jnp.sum(accv, axis=1, keepdims=True)

    def bisect_rows_on_lanes():
        for i in range(TOPM):
            candt_ref[i] = cand_ref[i].T

        def bit_step(it, carry):
            p, cnt_p = carry
            bit = 31 - it
            cand = p | (jnp.int32(1) << bit)
            thr_r = cand ^ jnp.int32(INT_MIN)
            accv = (candt_ref[0] >= thr_r).astype(i32)
            for i in range(1, TOPM):
                accv = accv + (candt_ref[i] >= thr_r).astype(i32)
            cnt = jnp.sum(accv, axis=0, keepdims=True)
            ok = cnt >= topk
            return jnp.where(ok, cand, p), jnp.where(ok, cnt, cnt_p)

        p_r, cnt_r = lax.fori_loop(0, 32, bit_step, (jnp.zeros((1, tq), i32), jnp.zeros((1, tq), i32)))
        eye = lax.broadcasted_iota(i32, (tq, tq), 0) == lax.broadcasted_iota(i32, (tq, tq), 1)

        def to_col(row):
            return jnp.sum(jnp.where(eye, jnp.broadcast_to(row, (tq, tq)), 0), axis=1, keepdims=True)

        return to_col(p_r), to_col(cnt_r)

    p_c, cnt_c = bisect_rows_on_lanes() if tq == LANES else bisect(count_cand_ge)
    thr_c = jnp.broadcast_to(p_c ^ jnp.int32(INT_MIN), (tq, LANES))
    min_kept = cand_ref[TOPM - 1]
    covered = (min_kept < thr_c) | (min_kept == jnp.int32(KEY_PAD))
    all_covered = jnp.min(covered.astype(f32)) > 0.0
    p_fin, cnt_ge = lax.cond(all_covered, lambda: (p_c, cnt_c),
                             lambda: bisect(lambda thr_b: count_where(lambda blk, _: blk >= thr_b)))
    thr = p_fin ^ jnp.int32(INT_MIN)
    thr_b = jnp.broadcast_to(thr, (tq, LANES))
    has_excess = cnt_ge > topk
    lane_i = lax.broadcasted_iota(i32, (tq, LANES), 1)

    def tie_limit():
        cnt_gt = count_where(lambda blk, _: blk > thr_b)
        need = topk - cnt_gt

        def idx_step(it, jj):
            bit = 14 - it
            cand = jj | (jnp.int32(1) << bit)
            cand_b = jnp.broadcast_to(cand, (tq, LANES))
            g = count_where(lambda blk, base: (blk == thr_b) & ((lane_i + base) < cand_b))
            return jnp.where(g < need, cand, jj)
        jj = lax.fori_loop(0, 15, idx_step, jnp.zeros((tq, 1), i32))
        return jnp.where(has_excess, jj, jnp.int32(INT_MAX))

    any_excess = jnp.max(has_excess.astype(f32)) > 0.0
    tie_j = lax.cond(any_excess, tie_limit, lambda: jnp.full((tq, 1), INT_MAX, i32))

    m_ref[...] = jnp.full(m_ref.shape, NEG_INF, f32)
    l_ref[...] = jnp.zeros(l_ref.shape, f32)
    acc_ref[...] = jnp.zeros(acc_ref.shape, f32)
    gq = A_GROUP * tq
    rc = min(tq, ATTN_ROW_CHUNK)

    def attn_block(kb, carry):
        off = pl.multiple_of(kb * kb_size, kb_size)
        key = keys_ref[kb]
        adm, kpos = valid_mask(kb)
        sel = (key > thr) | ((key == thr) & (kpos <= tie_j))
        mask = sel & adm
        if l_real < nkb_total * kb_size:
            mask = mask & (kpos < l_real)
        bias_ref[...] = jnp.where(mask, 0.0, NEG_INF)
        for g in range(A_KV_HEADS):
            qg = q_ref[g * A_GROUP:(g + 1) * A_GROUP].reshape(gq, LANES)
            kblk = k_vm[pl.ds(off, kb_size), g * LANES:(g + 1) * LANES]
            s_ref[g] = _dg(qg, kblk, ((1,), (1,)))
        for g in range(A_KV_HEADS):
            for r0 in range(0, gq, rc):
                qr0 = r0 % tq
                m_old = m_ref[g, r0:r0 + rc, :]
                parts = []
                mx = None
                for c in range(nchunk):
                    cs_ = slice(c * LANES, (c + 1) * LANES)
                    sc = s_ref[g, r0:r0 + rc, cs_] + bias_ref[qr0:qr0 + rc, cs_]
                    parts.append(sc)
                    mx = sc if mx is None else jnp.maximum(mx, sc)
                m_new = jnp.maximum(m_old, jnp.max(mx, axis=1, keepdims=True))
                alpha = jnp.exp2(m_old - m_new)
                psum = None
                for c in range(nchunk):
                    p = jnp.exp2(parts[c] - m_new)
                    p_ref[g, r0:r0 + rc, c * LANES:(c + 1) * LANES] = p.astype(bf16)
                    psum = p if psum is None else psum + p
                l_ref[g, r0:r0 + rc, :] = alpha * l_ref[g, r0:r0 + rc, :] + jnp.sum(psum, axis=1, keepdims=True)
                m_ref[g, r0:r0 + rc, :] = m_new
                alpha_ref[g, r0:r0 + rc, :] = alpha
            vblk = v_vm[pl.ds(off, kb_size), g * LANES:(g + 1) * LANES]
            acc_ref[g] = alpha_ref[g] * acc_ref[g] + jnp.dot(p_ref[g], vblk, preferred_element_type=f32)
        return carry

    lax.fori_loop(0, nkb, attn_block, 0)

    heads = []
    for g in range(A_KV_HEADS):
        og = acc_ref[g] / l_ref[g]
        for j in range(A_GROUP):
            heads.append(og[j * tq:(j + 1) * tq])
    o = jnp.concatenate(heads, axis=1)
    y = jnp.dot(o.astype(bf16), wo_ref[...], preferred_element_type=f32)
    o_ref[...] = _res_ln(x_ref[...], y, g_ref[...], b_ref[...])


def _dsa_attend(q_hm, qi_hm, wi, x2d, kb_all, vb_all, kib_all, w_out, g, b,
                *, n_batch, tq, kb_size, q_pos0, l_real, topk):
    R = x2d.shape[0]
    L = kb_all.shape[1]
    nq = R // (n_batch * tq)
    nkb_total = L // kb_size
    body = functools.partial(_dsa_body, tq=tq, kb_size=kb_size, nkb_total=nkb_total,
                             q_pos0=q_pos0, l_real=l_real, topk=topk)
    hm = pl.BlockSpec((A_HEADS, tq, LANES), lambda bi, qi: (0, bi * nq + qi, 0))
    row = lambda w: pl.BlockSpec((tq, w), lambda bi, qi: (bi * nq + qi, 0))
    anyspec = pl.BlockSpec(memory_space=pl.ANY)
    return pl.pallas_call(
        body,
        grid=(n_batch, nq),
        in_specs=[hm, hm, row(LANES), row(D_MODEL), anyspec, anyspec, anyspec,
                  _const_spec((A_QW, D_MODEL)), _const_spec((1, D_MODEL)), _const_spec((1, D_MODEL))],
        out_specs=row(D_MODEL),
        out_shape=jax.ShapeDtypeStruct((R, D_MODEL), f32),
        scratch_shapes=[
            pltpu.VMEM((L, A_KW), bf16),
            pltpu.VMEM((L, A_KW), bf16),
            pltpu.VMEM((L, LANES), bf16),
            pltpu.VMEM((nkb_total, tq, kb_size), i32),
            pltpu.VMEM((TOPM, tq, LANES), i32),
            pltpu.VMEM((TOPM, LANES, tq), i32),
            pltpu.VMEM((A_KV_HEADS, A_GROUP * tq, LANES), f32),
            pltpu.VMEM((A_KV_HEADS, A_GROUP * tq, LANES), f32),
            pltpu.VMEM((A_KV_HEADS, A_GROUP * tq, LANES), f32),
            pltpu.VMEM((A_KV_HEADS, A_GROUP * tq, LANES), f32),
            pltpu.VMEM((A_KV_HEADS, A_GROUP * tq, kb_size), f32),
            pltpu.VMEM((A_KV_HEADS, A_GROUP * tq, kb_size), bf16),
            pltpu.VMEM((tq, kb_size), f32),
            pltpu.SemaphoreType.DMA((3,)),
        ],
        compiler_params=_cparams(("arbitrary", "arbitrary")),
        name="dsa_attend",
    )(q_hm, qi_hm, wi, x2d, kb_all, vb_all, kib_all, w_out, g, b)


def _rglru_body(x_ref, h0_ref, cbuf_ref, win_ref, cw_ref, cb_ref, wa_ref, ba_ref, wx_ref, bx_ref,
                lam_ref, wout_ref, g_ref, b_ref,
                o_ref, hlast_ref, nbuf_ref,
                cconv_ref, hcar_ref, a_s, b_s, h_s, *, tb):
    t = pl.program_id(1)
    W = RNN_WIDTH

    @pl.when(t == 0)
    def _():
        cconv_ref[5:8, :] = cbuf_ref[0]
        hcar_ref[...] = h0_ref[0]

    x = x_ref[0]
    z = jnp.dot(x.astype(bf16), win_ref[...], preferred_element_type=f32)
    gate = z[:, :W]
    rec = z[:, W:]
    c3, c2, c1 = cconv_ref[5:6, :], cconv_ref[6:7, :], cconv_ref[7:8, :]
    r1, r2, r3 = _shift_rows(rec, [c3, c2, c1])
    u = (cw_ref[0:1, :] * r3 + cw_ref[1:2, :] * r2 + cw_ref[2:3, :] * r1 + cw_ref[3:4, :] * rec + cb_ref[...])
    cconv_ref[5:8, :] = rec[tb - 3:tb, :]
    nbuf_ref[0] = rec[tb - 3:tb, :]

    ub = u.astype(bf16)
    r = jax.nn.sigmoid(jnp.dot(ub, wa_ref[...], preferred_element_type=f32) + ba_ref[...])
    ig = jax.nn.sigmoid(jnp.dot(ub, wx_ref[...], preferred_element_type=f32) + bx_ref[...])
    log_a = (-LRU_C) * r * jax.nn.softplus(-lam_ref[...])
    a = jnp.exp(log_a)
    one_m_a2 = -jnp.tanh(log_a) * (a * a + 1.0)
    bv = jnp.sqrt(one_m_a2) * (ig * u)

    ng = tb // SUBLANES
    a3 = a.reshape(ng, SUBLANES, W)
    b3 = bv.reshape(ng, SUBLANES, W)
    sub = lax.broadcasted_iota(i32, (ng, SUBLANES, W), 1)
    off = 1
    while off < SUBLANES:
        m = sub >= off
        b3 = jnp.where(m, a3 * pltpu.roll(b3, off, axis=1) + b3, b3)
        a3 = jnp.where(m, a3 * pltpu.roll(a3, off, axis=1), a3)
        off *= 2
    a_s[...] = a3.reshape(tb, W)
    b_s[...] = b3.reshape(tb, W)

    def grp(gi, carry):
        r0 = pl.multiple_of(gi * SUBLANES, SUBLANES)
        hg = a_s[pl.ds(r0, SUBLANES), :] * carry + b_s[pl.ds(r0, SUBLANES), :]
        h_s[pl.ds(r0, SUBLANES), :] = hg
        return hg[SUBLANES - 1:SUBLANES, :]

    carry = lax.fori_loop(0, ng, grp, hcar_ref[...])
    hcar_ref[...] = carry
    hlast_ref[0] = carry

    y = jax.nn.gelu(gate) * h_s[...]
    out = jnp.dot(y.astype(bf16), wout_ref[...], preferred_element_type=f32)
    o_ref[0] = _res_ln(x, out, g_ref[...], b_ref[...])


def _mixer_b(x, h0, cbuf, w_in, conv_w, conv_b, wa, ba, wx, bx, lam, w_out, g, b, tb):
    B, T, _ = x.shape
    W = RNN_WIDTH
    nt = T // tb
    per_b = lambda r, c: pl.BlockSpec((1, r, c), lambda bi, ti: (bi, 0, 0))
    return pl.pallas_call(
        functools.partial(_rglru_body, tb=tb),
        grid=(B, nt),
        in_specs=[
            pl.BlockSpec((1, tb, D_MODEL), lambda bi, ti: (bi, ti, 0)),
            per_b(1, W), per_b(B_CONV - 1, W),
            _const_spec((D_MODEL, 2 * W)), _const_spec((B_CONV, W)), _const_spec((1, W)),
            _const_spec((W, W)), _const_spec((1, W)), _const_spec((W, W)), _const_spec((1, W)),
            _const_spec((1, W)), _const_spec((W, D_MODEL)), _const_spec((1, D_MODEL)), _const_spec((1, D_MODEL)),
        ],
        out_specs=[
            pl.BlockSpec((1, tb, D_MODEL), lambda bi, ti: (bi, ti, 0)),
            per_b(1, W), per_b(B_CONV - 1, W),
        ],
        out_shape=[jax.ShapeDtypeStruct((B, T, D_MODEL), f32),
                   jax.ShapeDtypeStruct((B, 1, W), f32),
                   jax.ShapeDtypeStruct((B, B_CONV - 1, W), f32)],
        scratch_shapes=[pltpu.VMEM((SUBLANES, W), f32), pltpu.VMEM((1, W), f32),
                        pltpu.VMEM((tb, W), f32), pltpu.VMEM((tb, W), f32), pltpu.VMEM((tb, W), f32)],
        compiler_params=_cparams(("arbitrary", "arbitrary")),
        name="rglru_mixer",
    )(x, h0, cbuf, w_in, conv_w, conv_b, wa, ba, wx, bx, lam, w_out, g, b)


C_HALO = 32


def _conformer_body(x_ref, cbuf_ref, win_ref, bin_ref, cw_ref, cb_ref, lg_ref, lb_ref, wout_ref, bout_ref,
                    g_ref, b_ref, o_ref, nbuf_ref, ext_ref, conv_ref, sh_ref, *, tb):
    t = pl.program_id(1)
    C = C_WIDTH
    hist = C_CONV - 1

    @pl.when(t == 0)
    def _():
        ext_ref[0:C_HALO - hist, :] = jnp.zeros((C_HALO - hist, C), f32)
        ext_ref[C_HALO - hist:C_HALO, :] = cbuf_ref[0]

    x = x_ref[0]
    z = jnp.dot(x.astype(bf16), win_ref[...], preferred_element_type=f32) + bin_ref[...]
    u = z[:, :C] * jax.nn.sigmoid(z[:, C:])
    ext_ref[C_HALO:C_HALO + tb, :] = u

    nsh = tb + C_HALO - SUBLANES
    for bres in range(1, SUBLANES):
        sh_ref[bres - 1] = ext_ref[pl.ds(bres, nsh), :]

    rt = min(tb, 128)
    for r0 in range(0, tb, rt):
        for c0 in range(0, C, LANES):
            acc = jnp.broadcast_to(cb_ref[:, c0:c0 + LANES], (rt, LANES))
            for k in range(C_CONV):
                a8, bres = divmod(k + C_HALO - hist, SUBLANES)
                if bres == 0:
                    win = ext_ref[pl.ds(r0 + SUBLANES * a8, rt), c0:c0 + LANES]
                else:
                    win = sh_ref[bres - 1, pl.ds(r0 + SUBLANES * a8, rt), c0:c0 + LANES]
                acc = acc + cw_ref[k:k + 1, c0:c0 + LANES] * win
            conv_ref[r0:r0 + rt, c0:c0 + LANES] = acc

    nbuf_ref[0] = ext_ref[tb + C_HALO - hist:tb + C_HALO, :]
    ext_ref[0:C_HALO, :] = ext_ref[tb:tb + C_HALO, :]

    c = _layer_norm(conv_ref[...], lg_ref[...], lb_ref[...])
    s = jax.nn.silu(c)
    out = jnp.dot(s.astype(bf16), wout_ref[...], preferred_element_type=f32) + bout_ref[...]
    o_ref[0] = _res_ln(x, out, g_ref[...], b_ref[...])


def _mixer_c(x, cbuf, w_in, b_in, conv_w, conv_b, ln_g, ln_b, w_out, b_out, g, b, tb):
    B, T, _ = x.shape
    C = C_WIDTH
    nt = T // tb
    return pl.pallas_call(
        functools.partial(_conformer_body, tb=tb),
        grid=(B, nt),
        in_specs=[
            pl.BlockSpec((1, tb, D_MODEL), lambda bi, ti: (bi, ti, 0)),
            pl.BlockSpec((1, C_CONV - 1, C), lambda bi, ti: (bi, 0, 0)),
            _const_spec((D_MODEL, 2 * C)), _const_spec((1, 2 * C)),
            _const_spec((C_CONV, C)), _const_spec((1, C)), _const_spec((1, C)), _const_spec((1, C)),
            _const_spec((C, D_MODEL)), _const_spec((1, D_MODEL)),
            _const_spec((1, D_MODEL)), _const_spec((1, D_MODEL)),
        ],
        out_specs=[
            pl.BlockSpec((1, tb, D_MODEL), lambda bi, ti: (bi, ti, 0)),
            pl.BlockSpec((1, C_CONV - 1, C), lambda bi, ti: (bi, 0, 0)),
        ],
        out_shape=[jax.ShapeDtypeStruct((B, T, D_MODEL), f32),
                   jax.ShapeDtypeStruct((B, C_CONV - 1, C), f32)],
        scratch_shapes=[pltpu.VMEM((tb + C_HALO, C), f32), pltpu.VMEM((tb, C), f32),
                        pltpu.VMEM((SUBLANES - 1, tb + C_HALO - SUBLANES, C), f32)],
        compiler_params=_cparams(("arbitrary", "arbitrary")),
        name="conformer_mixer",
    )(x, cbuf, w_in, b_in, conv_w, conv_b, ln_g, ln_b, w_out, b_out, g, b)


def _rwkv_front_body(x_ref, sh_ref, mu_ref, wr_ref, wk_ref, wv_ref, w0_ref, w1_ref, w2_ref,
                     a0_ref, a1_ref, a2_ref, g1_ref, g2_ref, kk_ref, ka_ref, mseg_ref,
                     r_o, lw_o, k_o, v_o, an_o, bk_o, g_o, shout_ref, car_ref, *, tb):
    t = pl.program_id(1)

    @pl.when(t == 0)
    def _():
        car_ref[...] = sh_ref[0]

    x = x_ref[0]
    (prev,) = _shift_rows(x, [car_ref[...]])
    car_ref[...] = x[tb - 1:tb, :]
    shout_ref[0] = x[tb - 1:tb, :]
    xx = prev - x
    xr, xw, xk, xv, xa, xg = (x + xx * mu_ref[n:n + 1, :] for n in range(6))
    r = _mm(xr, wr_ref[...])
    k = _mm(xk, wk_ref[...])
    v = _mm(xv, wv_ref[...])
    wl = w0_ref[...] + _mm(jnp.tanh(_mm(xw, w1_ref[...])), w2_ref[...])
    w_log = -jax.nn.softplus(-wl) - 0.5
    lw = -jnp.exp(w_log)
    a = jax.nn.sigmoid(a0_ref[...] + _mm(_mm(xa, a1_ref[...]), a2_ref[...]))
    g = _mm(jax.nn.sigmoid(_mm(xg, g1_ref[...])), g2_ref[...])
    kk = k * kk_ref[...]
    n2 = _dot2(kk * kk, mseg_ref[...])
    kk = kk / jnp.maximum(jnp.sqrt(n2), 1e-12)
    k2 = k * (1.0 + (a - 1.0) * ka_ref[...])
    outs = ((r_o, r), (lw_o, lw), (k_o, k2), (v_o, v), (an_o, -kk), (bk_o, kk * a), (g_o, g))
    for ref, val in outs:
        for p in range(D_PAIRS):
            ref[0, p] = val[:, p * LANES:(p + 1) * LANES]


def _rwkv_front(x, shift0, mu, w_r, w_k, w_v, w0, w1, w2, a0, a1, a2, g1, g2, k_k, k_a, mseg, tb):
    B, T, D = x.shape
    nt = T // tb
    pm = pl.BlockSpec((1, D_PAIRS, tb, LANES), lambda bi, ti: (bi, 0, ti, 0))
    pm_shape = jax.ShapeDtypeStruct((B, D_PAIRS, T, LANES), f32)
    vec = _const_spec((1, D))
    return pl.pallas_call(
        functools.partial(_rwkv_front_body, tb=tb),
        grid=(B, nt),
        in_specs=[
            pl.BlockSpec((1, tb, D), lambda bi, ti: (bi, ti, 0)),
            pl.BlockSpec((1, 1, D), lambda bi, ti: (bi, 0, 0)),
            _const_spec((6, D)), _const_spec((D, D)), _const_spec((D, D)), _const_spec((D, D)),
            vec, _const_spec((D, LORA_PAD)), _const_spec((LORA_PAD, D)),
            vec, _const_spec((D, LORA_PAD)), _const_spec((LORA_PAD, D)),
            _const_spec((D, LORA_PAD)), _const_spec((LORA_PAD, D)),
            vec, vec, _const_spec((D, D)),
        ],
        out_specs=[pm] * 7 + [pl.BlockSpec((1, 1, D), lambda bi, ti: (bi, 0, 0))],
        out_shape=[pm_shape] * 7 + [jax.ShapeDtypeStruct((B, 1, D), f32)],
        scratch_shapes=[pltpu.VMEM((1, D), f32)],
        compiler_params=_cparams(("arbitrary", "arbitrary")),
        name="rwkv_front",
    )(x, shift0, mu, w_r, w_k, w_v, w0, w1, w2, a0, a1, a2, g1, g2, k_k, k_a, mseg)


def _wkv_body(r_ref, lw_ref, k_ref, v_ref, an_ref, bk_ref, g_ref, s0_ref, rk_ref, gng_ref, gnb_ref,
              y_ref, sout_ref, s_ref, *, tb):
    t = pl.program_id(1)
    C = WKV_CHUNK
    P2 = 2 * C

    @pl.when(t == 0)
    def _():
        s_ref[...] = s0_ref[0]

    lane = lax.broadcasted_iota(i32, (1, LANES), 1)
    m0 = (lane < D_HEAD_DIM).astype(f32)
    m1 = 1.0 - m0
    ri = lax.broadcasted_iota(i32, (P2, P2), 0)
    ci = lax.broadcasted_iota(i32, (P2, P2), 1)
    same_head = (ri >= C) == (ci >= C)
    s_idx = ri & (C - 1)
    t_idx = ci & (C - 1)
    mask_strict = same_head & (s_idx < t_idx)
    mask_incl = same_head & (s_idx <= t_idx)
    eye = (ri == ci).astype(f32)
    seg_ones = ((ri >= D_HEAD_DIM) == (ci >= D_HEAD_DIM)).astype(f32)
    tr = lax.broadcasted_iota(i32, (C, C), 0)
    tc = lax.broadcasted_iota(i32, (C, C), 1)
    tri_incl = (tc <= tr).astype(f32)

    def stack2(z):
        return jnp.concatenate([z * m0, z * m1], axis=0)

    dims_nt = ((1,), (1,))
    dims_tn = ((0,), (0,))
    inv = 1.0 / D_HEAD_DIM
    PR = range(D_PAIRS)
    units = [(c, p) for c in range(tb // C) for p in PR]
    UN = range(len(units))
    sls = [slice(c * C, (c + 1) * C) for c, _ in units]
    prs = [p for _, p in units]
    r = [r_ref[0, prs[u], sls[u], :] for u in UN]
    lw = [lw_ref[0, prs[u], sls[u], :] for u in UN]
    k = [k_ref[0, prs[u], sls[u], :] for u in UN]
    v = [v_ref[0, prs[u], sls[u], :] for u in UN]
    an = [an_ref[0, prs[u], sls[u], :] for u in UN]
    bk = [bk_ref[0, prs[u], sls[u], :] for u in UN]
    cs = [_dot2l(tri_incl, lw[u]) for u in UN]
    cs_end = [cs[u][C - 1:C, :] for u in UN]
    ah = [an[u] * jnp.exp(cs[u] - lw[u]) for u in UN]
    p_inv = [jnp.exp(-cs[u]) for u in UN]
    bh = [bk[u] * p_inv[u] for u in UN]
    kh = [k[u] * p_inv[u] for u in UN]
    rh = [r[u] * jnp.exp(cs[u]) for u in UN]
    p_tail = [jnp.exp(cs_end[u] - cs[u]) for u in UN]
    X = [jnp.concatenate([stack2(bh[u]), stack2(kh[u])], axis=0) for u in UN]
    Ga = [_mm_nt(X[u], jnp.concatenate([ah[u], ah[u]], axis=0)) for u in UN]
    Gr = [_mm_nt(X[u], jnp.concatenate([rh[u], rh[u]], axis=0)) for u in UN]
    n_ab = [jnp.where(mask_strict, Ga[u][0:P2], 0.0) for u in UN]
    n_ak = [jnp.where(mask_strict, Ga[u][P2:2 * P2], 0.0) for u in UN]
    n_rb = [jnp.where(mask_incl, Gr[u][0:P2], 0.0).astype(bf16) for u in UN]
    n_rk = [jnp.where(mask_incl, Gr[u][P2:2 * P2], 0.0).astype(bf16) for u in UN]
    tt = [eye + n_ab[u] for u in UN]
    mpow = n_ab
    steps = 1
    while steps * 2 < C:
        mpow = [_mm(mpow[u], mpow[u]) for u in UN]
        tt = [tt[u] + _mm(tt[u], mpow[u]) for u in UN]
        steps *= 2
    a2 = [stack2(ah[u]) for u in UN]
    r2 = [stack2(rh[u]) for u in UN]
    v2 = [stack2(v[u]) for u in UN]
    for c in range(tb // C):
        CU = [u for u in UN if units[u][0] == c]
        S = {u: s_ref[prs[u]] for u in CU}
        rhs = {u: _dot3(a2[u], S[u], dims_nt) + _dot3(n_ak[u], v2[u], dims_tn) for u in CU}
        u2 = {u: _dot3(tt[u], rhs[u], dims_tn) for u in CU}
        for u in CU:
            bt2 = stack2(bk[u] * p_tail[u])
            kt2 = stack2(k[u] * p_tail[u])
            s_ref[prs[u]] = (S[u] * jnp.exp(cs_end[u]) + _dot3(u2[u], bt2, dims_tn)
                             + _dot3(v2[u], kt2, dims_tn))
        y2 = {u: _mm_nt(r2[u], S[u]) + _dg(n_rb[u], u2[u].astype(bf16), dims_tn)
              + _dg(n_rk[u], v2[u].astype(bf16), dims_tn) for u in CU}
        for u in CU:
            p = prs[u]
            y = y2[u][0:C] + y2[u][C:P2]
            mu_y = _dot2(y, seg_ones) * inv
            d = y - mu_y
            var = _dot2(d * d, seg_ones) * inv
            yn = d * lax.rsqrt(var + GN_EPS) * gng_ref[p] + gnb_ref[p]
            bonus = _dot2(r[u] * k[u] * rk_ref[p], seg_ones) * v[u]
            y_ref[0, p, sls[u], :] = (yn + bonus) * g_ref[0, p, sls[u], :]

    sout_ref[0] = s_ref[...]


def _wkv(r, lw, k, v, an, bk, g, s0, r_k, gn_g, gn_b, tb):
    B, NP, T, _ = r.shape
    nt = T // tb
    pm = pl.BlockSpec((1, NP, tb, LANES), lambda bi, ti: (bi, 0, ti, 0))
    st = pl.BlockSpec((1, NP, LANES, LANES), lambda bi, ti: (bi, 0, 0, 0))
    par = _const_spec((NP, 1, LANES))
    return pl.pallas_call(
        functools.partial(_wkv_body, tb=tb),
        grid=(B, nt),
        in_specs=[pm] * 7 + [st, par, par, par],
        out_specs=[pm, st],
        out_shape=[jax.ShapeDtypeStruct((B, NP, T, LANES), f32),
                   jax.ShapeDtypeStruct((B, NP, LANES, LANES), f32)],
        scratch_shapes=[pltpu.VMEM((NP, LANES, LANES), f32)],
        compiler_params=_cparams(("arbitrary", "arbitrary")),
        name="rwkv_wkv",
    )(r, lw, k, v, an, bk, g, s0, r_k, gn_g, gn_b)


def _rwkv_out_body(y_ref, x_ref, wo_ref, g_ref, b_ref, o_ref):
    y = jnp.concatenate([y_ref[0, p] for p in range(D_PAIRS)], axis=1)
    out = jnp.dot(y.astype(bf16), wo_ref[...], preferred_element_type=f32)
    o_ref[0] = _res_ln(x_ref[0], out, g_ref[...], b_ref[...])


def _rwkv_out(y_pm, x, w_o, g, b, tb):
    B, T, D = x.shape
    nt = T // tb
    return pl.pallas_call(
        _rwkv_out_body,
        grid=(B, nt),
        in_specs=[pl.BlockSpec((1, D_PAIRS, tb, LANES), lambda bi, ti: (bi, 0, ti, 0)),
                  pl.BlockSpec((1, tb, D), lambda bi, ti: (bi, ti, 0)),
                  _const_spec((D, D)), _const_spec((1, D)), _const_spec((1, D))],
        out_specs=pl.BlockSpec((1, tb, D), lambda bi, ti: (bi, ti, 0)),
        out_shape=jax.ShapeDtypeStruct((B, T, D), f32),
        compiler_params=_cparams(("arbitrary", "arbitrary")),
        name="rwkv_out",
    )(y_pm, x, w_o, g, b)


def _rope_tables(pos, head_dim, reps):
    half = head_dim // 2
    inv_freq = jnp.exp(-math.log(ROPE_THETA) * jnp.arange(half, dtype=f32) / half)
    ang = pos.astype(f32)[:, None] * inv_freq[None, :]
    cos, sin = jnp.cos(ang), jnp.sin(ang)
    cos_t = jnp.tile(jnp.concatenate([cos, cos], axis=1), (1, reps))
    sin_t = jnp.tile(jnp.concatenate([-sin, sin], axis=1), (1, reps))
    return cos_t, sin_t


def _pad_a_w_in(w):
    q = w[:, :A_QW]
    k = w[:, A_QW:A_QW + A_KW]
    v = w[:, A_QW + A_KW:A_QW + 2 * A_KW]
    o = A_QW + 2 * A_KW
    qi = w[:, o:o + IDX_HEADS * IDX_DIM].reshape(D_MODEL, IDX_HEADS, IDX_DIM)
    qi = jnp.pad(qi, ((0, 0), (0, 0), (0, LANES - IDX_DIM))).reshape(D_MODEL, IDX_HEADS * LANES)
    o += IDX_HEADS * IDX_DIM
    ki = jnp.pad(w[:, o:o + IDX_DIM], ((0, 0), (0, LANES - IDX_DIM)))
    o += IDX_DIM
    wi = jnp.pad(w[:, o:o + IDX_HEADS], ((0, 0), (0, LANES - IDX_HEADS)))
    return jnp.concatenate([q, k, v, qi, ki, wi], axis=1).astype(bf16)


def _block_diag(w):
    n, d, _ = w.shape
    eye = jnp.eye(n, dtype=w.dtype)
    return (eye[:, None, :, None] * w[:, :, None, :]).reshape(n * d, n * d)


def _pair_state(s):
    B = s.shape[0]
    s = s.reshape(B, D_PAIRS, 2, D_HEAD_DIM, D_HEAD_DIM)
    z = jnp.zeros_like(s[:, :, 0])
    top = jnp.concatenate([s[:, :, 0], z], axis=-1)
    bot = jnp.concatenate([z, s[:, :, 1]], axis=-1)
    return jnp.concatenate([top, bot], axis=-2)


def _unpair_state(sp):
    B = sp.shape[0]
    h0 = sp[:, :, :D_HEAD_DIM, :D_HEAD_DIM]
    h1 = sp[:, :, D_HEAD_DIM:, D_HEAD_DIM:]
    return jnp.stack([h0, h1], axis=2).reshape(B, D_HEADS, D_HEAD_DIM, D_HEAD_DIM)


def _row(v):
    return v.reshape(1, -1).astype(f32)


def _mixer_a_stream(x, w_pad, w_out, g, b, pos, cache, *, tb, tq, kb_size):
    B, T, D = x.shape
    R = B * T
    c128, s128 = _rope_tables(pos, A_HEAD_DIM, 1)
    c64, s64 = _rope_tables(pos, IDX_DIM, 2)
    if tb > T:
        rep = tb // T
        c128, s128, c64, s64 = (jnp.tile(a, (rep, 1)) for a in (c128, s128, c64, s64))
    x2d = x.reshape(R, D)
    q_hm, k_o, v_o, kb, vb, qi_hm, ki_o, kib, wi = _a_project(x2d, w_pad, c128, s128, c64, s64, tb)
    kb3, vb3, kib3 = kb.reshape(B, T, A_KW), vb.reshape(B, T, A_KW), kib.reshape(B, T, LANES)
    if cache is None:
        q_pos0, l_real = 0, T
    else:
        ck, cv, cki = cache
        P = ck.shape[1]
        kb3 = jnp.concatenate([ck.reshape(B, P, A_KW).astype(bf16), kb3], axis=1)
        vb3 = jnp.concatenate([cv.reshape(B, P, A_KW).astype(bf16), vb3], axis=1)
        cki_p = jnp.pad(cki, ((0, 0), (0, 0), (0, LANES - IDX_DIM))).astype(bf16)
        kib3 = jnp.concatenate([cki_p, kib3], axis=1)
        q_pos0, l_real = P, P + T
    l_pad = -(-l_real // kb_size) * kb_size
    if l_pad > l_real:
        padk = ((0, 0), (0, l_pad - l_real), (0, 0))
        kb3, vb3, kib3 = jnp.pad(kb3, padk), jnp.pad(vb3, padk), jnp.pad(kib3, padk)
    topk = min(TOPK_MAX, l_real // 4)
    xn = _dsa_attend(q_hm, qi_hm, wi, x2d, kb3, vb3, kib3, w_out, g, b,
                     n_batch=B, tq=tq, kb_size=kb_size, q_pos0=q_pos0, l_real=l_real, topk=topk)
    return (xn.reshape(B, T, D), k_o.reshape(B, T, A_KV_HEADS, A_HEAD_DIM),
            v_o.reshape(B, T, A_KV_HEADS, A_HEAD_DIM), ki_o.reshape(B, T, IDX_DIM))


def _mixer_d_stream(x, wkv0, shift0, prm, g, b, *, tb, tb_wkv):
    B, T, D = x.shape
    outs = _rwkv_front(x, shift0.reshape(B, 1, D), prm["mu"], prm["w_r"], prm["w_k"], prm["w_v"], prm["w0"],
                       prm["w1"], prm["w2"], prm["a0"], prm["a1"], prm["a2"], prm["g1"], prm["g2"],
                       prm["k_k"], prm["k_a"], prm["mseg"], tb)
    seq, shift_out = outs[:7], outs[7]
    t_pad = -(-T // tb_wkv) * tb_wkv
    if t_pad > T:
        seq = [jnp.pad(a, ((0, 0), (0, 0), (0, t_pad - T), (0, 0))) for a in seq]
    y_pm, s_out = _wkv(*seq, _pair_state(wkv0), prm["r_k"], prm["gn_g"], prm["gn_b"], tb_wkv)
    if t_pad > T:
        y_pm = y_pm[:, :, :T]
    xn = _rwkv_out(y_pm, x, prm["w_o"], g, b, tb)
    return xn, _unpair_state(s_out), shift_out.reshape(B, D)


def kernel(x_prompt, x_sample, cache_a_k, cache_a_v, cache_a_kidx, state_b_h, state_b_conv, state_c_conv,
           state_d_wkv, state_d_shift, state_f_conv, ln_g, ln_b, a_w_in, a_w_out,
           b_w_in, b_conv_w, b_conv_b, b_gate_a_w, b_gate_a_b, b_gate_x_w, b_gate_x_b, b_lambda, b_w_out,
           c_w_in, c_b_in, c_conv_w, c_conv_b, c_ln_g, c_ln_b, c_w_out, c_b_out,
           d_mu, d_w_r, d_w_k, d_w_v, d_w_o, d_w0, d_w1, d_w2, d_a0, d_a1, d_a2, d_g1, d_g2,
           d_k_k, d_k_a, d_r_k, d_gn_g, d_gn_b, f_w_up, f_conv_w, f_conv_b, f_w_down):
    xp, xs = x_prompt, x_sample
    BP, TP, D = xp.shape
    BS, TS, _ = xs.shape
    tb_p = 256
    tb_s = TS
    outs = {k: [] for k in ("ak_p", "ak_s", "av_p", "av_s", "aki_p", "aki_s", "bh_p", "bh_s", "bc_p", "bc_s",
                            "cc_p", "cc_s", "dw_p", "dw_s", "dsh_p", "dsh_s", "fc_p", "fc_s")}
    for i in range(DEPTH):
        m, j = i % 4, i // 4
        g0, b0 = _row(ln_g[i, 0]), _row(ln_b[i, 0])
        g1, b1 = _row(ln_g[i, 1]), _row(ln_b[i, 1])
        if m == 0:
            w_pad = _pad_a_w_in(a_w_in[j])
            w_out = a_w_out[j].astype(bf16)
            xp, kp, vp, kip = _mixer_a_stream(xp, w_pad, w_out, g0, b0, jnp.arange(TP, dtype=i32), None,
                                              tb=512, tq=128, kb_size=1024)
            P = cache_a_k.shape[2]
            xs, ks, vs, kis = _mixer_a_stream(xs, w_pad, w_out, g0, b0, P + jnp.arange(TS, dtype=i32),
                                              (cache_a_k[j], cache_a_v[j], cache_a_kidx[j]),
                                              tb=BS * TS, tq=TS, kb_size=256)
            outs["ak_p"].append(kp); outs["ak_s"].append(ks)
            outs["av_p"].append(vp); outs["av_s"].append(vs)
            outs["aki_p"].append(kip); outs["aki_s"].append(kis)
        elif m == 1:
            W = RNN_WIDTH
            args = (b_w_in[j].astype(bf16), b_conv_w[j], _row(b_conv_b[j]),
                    _block_diag(b_gate_a_w[j]).astype(bf16), _row(b_gate_a_b[j]),
                    _block_diag(b_gate_x_w[j]).astype(bf16), _row(b_gate_x_b[j]),
                    _row(b_lambda[j]), b_w_out[j].astype(bf16), g0, b0)
            xp, hp, cp = _mixer_b(xp, jnp.zeros((BP, 1, W), f32), jnp.zeros((BP, B_CONV - 1, W), f32), *args, tb_p)
            xs, hs, cs = _mixer_b(xs, state_b_h[j].reshape(BS, 1, W), state_b_conv[j], *args, tb_s)
            outs["bh_p"].append(hp.reshape(BP, W)); outs["bh_s"].append(hs.reshape(BS, W))
            outs["bc_p"].append(cp); outs["bc_s"].append(cs)
        elif m == 2:
            args = (c_w_in[j].astype(bf16), _row(c_b_in[j]), c_conv_w[j], _row(c_conv_b[j]),
                    _row(c_ln_g[j]), _row(c_ln_b[j]), c_w_out[j].astype(bf16), _row(c_b_out[j]), g0, b0)
            xp, cp = _mixer_c(xp, jnp.zeros((BP, C_CONV - 1, C_WIDTH), f32), *args, tb_p)
            xs, cs = _mixer_c(xs, state_c_conv[j], *args, tb_s)
            outs["cc_p"].append(cp); outs["cc_s"].append(cs)
        else:
            lp = ((0, 0), (0, LORA_PAD - d_w1.shape[2]))
            lq = ((0, LORA_PAD - d_w1.shape[2]), (0, 0))
            gp = ((0, 0), (0, LORA_PAD - d_g1.shape[2]))
            gq = ((0, LORA_PAD - d_g1.shape[2]), (0, 0))
            pairs = lambda v: v.reshape(D_PAIRS, 1, LANES).astype(f32)
            prm = dict(
                mu=d_mu[j], w_r=d_w_r[j].astype(bf16), w_k=d_w_k[j].astype(bf16), w_v=d_w_v[j].astype(bf16),
                w_o=d_w_o[j].astype(bf16), w0=_row(d_w0[j]),
                w1=jnp.pad(d_w1[j], lp).astype(bf16), w2=jnp.pad(d_w2[j], lq).astype(bf16),
                a0=_row(d_a0[j]), a1=jnp.pad(d_a1[j], lp).astype(bf16), a2=jnp.pad(d_a2[j], lq).astype(bf16),
                g1=jnp.pad(d_g1[j], gp).astype(bf16), g2=jnp.pad(d_g2[j], gq).astype(bf16),
                k_k=_row(d_k_k[j]), k_a=_row(d_k_a[j]),
                mseg=_block_diag(jnp.ones((D_HEADS, D_HEAD_DIM, D_HEAD_DIM), bf16)),
                r_k=pairs(d_r_k[j]), gn_g=pairs(d_gn_g[j]), gn_b=pairs(d_gn_b[j]))
            xp, sp, shp = _mixer_d_stream(xp, jnp.zeros((BP, D_HEADS, D_HEAD_DIM, D_HEAD_DIM), f32),
                                          jnp.zeros((BP, D), f32), prm, g0, b0, tb=tb_p, tb_wkv=2 * WKV_CHUNK)
            xs, ss, shs = _mixer_d_stream(xs, state_d_wkv[j], state_d_shift[j], prm, g0, b0,
                                          tb=tb_s, tb_wkv=WKV_CHUNK)
            outs["dw_p"].append(sp); outs["dw_s"].append(ss)
            outs["dsh_p"].append(shp); outs["dsh_s"].append(shs)
        fargs = (f_w_up[i].astype(bf16), f_conv_w[i], _row(f_conv_b[i]), f_w_down[i].astype(bf16), g1, b1)
        xp, fbp = _conv_ffn(xp, jnp.zeros((BP, F_CONV - 1, 2 * D_FF), f32), *fargs, tb_p)
        xs, fbs = _conv_ffn(xs, state_f_conv[i], *fargs, tb_s)
        outs["fc_p"].append(fbp); outs["fc_s"].append(fbs)
    st = lambda k: jnp.stack(outs[k])
    return (xp, xs,
            st("ak_p"), st("ak_s"), st("av_p"), st("av_s"), st("aki_p"), st("aki_s"),
            st("bh_p"), st("bh_s"), st("bc_p"), st("bc_s"), st("cc_p"), st("cc_s"),
            st("dw_p"), st("dw_s"), st("dsh_p"), st("dsh_s"), st("fc_p"), st("fc_s"))
```
